```python
import jax, jax.numpy as jnp
from jax import lax
import numpy as np

D_MODEL = 2048
BATCH = 4
SEQ = 8192
DEPTH = 2
DEC_BATCH = 8
DEC_SEQ = 16
PAST_LEN = 2048

CHUNK = 64
Q_BLOCK = 128
N_HEADS = D_MODEL // 128
Q_LORA = 512
KV_LORA = 512
NOPE_DIM = 128
ROPE_DIM = 64
V_DIM = 128
QK_DIM = NOPE_DIM + ROPE_DIM
ROPE_THETA = 10000.0
D_MIX = N_HEADS * V_DIM
D_LRU = D_MODEL
LRU_BLOCKS = D_LRU // 128
LRU_BW = D_LRU // LRU_BLOCKS
CONV_W = 4
LRU_C = 8.0
D_FF = 5504
N_EXPERTS = 8
TOP_K = 2
D_FF_EXPERT = 7168
N_DENSE = (DEPTH + 1) // 2
N_MOE = DEPTH // 2
DN_ALPHA = (2 * DEPTH) ** 0.25
DN_BETA = (8 * DEPTH) ** -0.25
LN_EPS = 1e-5
RMS_EPS = 1e-6
IN_SPLITS = (Q_LORA, KV_LORA, ROPE_DIM, D_LRU, D_LRU, D_MODEL, D_MODEL)
D_IN = sum(IN_SPLITS)
IN_OFFSETS = tuple(int(v) for v in np.cumsum(IN_SPLITS)[:-1])

kernel_name = 'hybrid_rglru_mla_streaming_step'


def layer_norm(x, g, b):
    xf = x.astype(jnp.float32)
    mu = jnp.mean(xf, -1, keepdims=True)
    var = jnp.mean(jnp.square(xf - mu), -1, keepdims=True)
    return ((xf - mu) * lax.rsqrt(var + LN_EPS) * g + b).astype(x.dtype)


def rms_norm(x, g):
    xf = x.astype(jnp.float32)
    return (xf * lax.rsqrt(jnp.mean(xf * xf, -1, keepdims=True) + RMS_EPS) * g).astype(x.dtype)


def rope(x, pos):
    half = ROPE_DIM // 2
    inv = ROPE_THETA ** (-jnp.arange(half, dtype=jnp.float32) / half)
    ang = pos.astype(jnp.float32)[:, None] * inv[None, :]
    cos = jnp.cos(ang)[:, None, :]
    sin = jnp.sin(ang)[:, None, :]
    xf = x.astype(jnp.float32)
    x1, x2 = xf[..., :half], xf[..., half:]
    return jnp.concatenate([x1 * cos - x2 * sin, x1 * sin + x2 * cos], -1).astype(x.dtype)


def attend_block(q_nope, q_rope, q_pos, k_nope, k_rope, v, k_pos):
    s = (jnp.einsum('bqhn,bkhn->bhqk', q_nope, k_nope)
         + jnp.einsum('bqhr,bkr->bhqk', q_rope, k_rope)).astype(jnp.float32) * (QK_DIM ** -0.5)
    mask = (k_pos[None, :] // CHUNK) <= (q_pos[:, None] // CHUNK)
    p = jax.nn.softmax(jnp.where(mask, s, -jnp.inf), axis=-1)
    return jnp.einsum('bhqk,bkhv->bqhv', p.astype(v.dtype), v)


def mla_attention(q_nope, q_rope, q_pos, k_nope, k_rope, v, k_pos):
    B, S = q_nope.shape[:2]
    if S % Q_BLOCK:
        return attend_block(q_nope, q_rope, q_pos, k_nope, k_rope, v, k_pos)
    nb = S // Q_BLOCK
    qn = q_nope.reshape(B, nb, Q_BLOCK, N_HEADS, NOPE_DIM).swapaxes(0, 1)
    qr = q_rope.reshape(B, nb, Q_BLOCK, N_HEADS, ROPE_DIM).swapaxes(0, 1)
    qp = q_pos.reshape(nb, Q_BLOCK)

    def blk(args):
        a, r, p = args
        return attend_block(a, r, p, k_nope, k_rope, v, k_pos)

    out = lax.map(blk, (qn, qr, qp))
    return out.swapaxes(0, 1).reshape(B, S, N_HEADS, V_DIM)


def causal_conv(u, buf, w, b):
    S = u.shape[1]
    uc = jnp.concatenate([buf, u], axis=1)
    y = b
    for j in range(CONV_W):
        y = y + uc[:, j:j + S] * w[j]
    return y, uc[:, -(CONV_W - 1):]


def rg_lru(u, h0, w_a, b_a, w_x, b_x, lam):
    B, S, W = u.shape
    ub = u.reshape(B, S, LRU_BLOCKS, LRU_BW)
    r = jax.nn.sigmoid(jnp.einsum('bsnc,ncd->bsnd', ub, w_a) + b_a).reshape(B, S, W)
    i = jax.nn.sigmoid(jnp.einsum('bsnc,ncd->bsnd', ub, w_x) + b_x).reshape(B, S, W)
    log_a = (-LRU_C * r.astype(jnp.float32)) * jax.nn.softplus(-lam.astype(jnp.float32))
    a = jnp.exp(log_a)
    bt = jnp.sqrt(-jnp.expm1(2.0 * log_a)) * (i * u).astype(jnp.float32)
    bt = bt.at[:, 0].add(a[:, 0] * h0.astype(jnp.float32))

    def combine(lhs, rhs):
        a1, b1 = lhs
        a2, b2 = rhs
        return a1 * a2, a2 * b1 + b2

    _, h = lax.associative_scan(combine, (a, bt), axis=1)
    h = h.astype(u.dtype)
    return h, h[:, -1]


def temporal_mix(x, pos, past_lat, past_kr, h0, conv_buf, w_in, q_norm_g, w_uq, kv_norm_g, w_uk, w_uv,
                 conv_w, conv_b, w_gate_a, b_gate_a, w_gate_x, b_gate_x, lru_lambda, w_o):
    B, S, _ = x.shape
    proj = jnp.einsum('bsd,de->bse', x, w_in)
    c_q, c_kv, k_r, u, lru_gate, g_a, g_b = jnp.split(proj, IN_OFFSETS, axis=-1)
    q = jnp.einsum('bsc,ce->bse', rms_norm(c_q, q_norm_g), w_uq).reshape(B, S, N_HEADS, QK_DIM)
    q_nope = q[..., :NOPE_DIM]
    q_rope = rope(q[..., NOPE_DIM:], pos)
    lat = rms_norm(c_kv, kv_norm_g)
    k_rope = rope(k_r[:, :, None, :], pos)[:, :, 0]
    if past_lat is None:
        lat_all, kr_all, k_pos = lat, k_rope, pos
    else:
        lat_all = jnp.concatenate([past_lat, lat], axis=1)
        kr_all = jnp.concatenate([past_kr, k_rope], axis=1)
        k_pos = jnp.concatenate([jnp.arange(past_lat.shape[1], dtype=pos.dtype), pos])
    K = lat_all.shape[1]
    k_nope = jnp.einsum('bkc,ce->bke', lat_all, w_uk).reshape(B, K, N_HEADS, NOPE_DIM)
    v = jnp.einsum('bkc,ce->bke', lat_all, w_uv).reshape(B, K, N_HEADS, V_DIM)
    attn = mla_attention(q_nope, q_rope, pos, k_nope, kr_all, v, k_pos).reshape(B, S, D_MIX)
    u_c, new_buf = causal_conv(u, conv_buf, conv_w, conv_b)
    h, h_last = rg_lru(u_c, h0, w_gate_a, b_gate_a, w_gate_x, b_gate_x, lru_lambda)
    lru_out = h * jax.nn.gelu(lru_gate)
    merged = jax.nn.sigmoid(g_a) * lru_out + jax.nn.sigmoid(g_b) * attn
    out = jnp.einsum('bse,ed->bsd', merged, w_o)
    return out, lat, k_rope, h_last, new_buf


def swiglu(x, w1, w3, w2):
    return jnp.matmul(jax.nn.silu(jnp.matmul(x, w1)) * jnp.matmul(x, w3), w2)


def moe(x, w_router, w1, w3, w2):
    logits = jnp.einsum('bsd,de->bse', x, w_router).astype(jnp.float32)
    top_v, top_i = lax.top_k(logits, TOP_K)
    gates = jax.nn.softmax(top_v, axis=-1)
    comb = jnp.sum(jax.nn.one_hot(top_i, N_EXPERTS, dtype=jnp.float32) * gates[..., None], axis=-2)
    y = jnp.zeros_like(x)
    for e in range(N_EXPERTS):
        y = y + comb[..., e:e + 1].astype(x.dtype) * swiglu(x, w1[e], w3[e], w2[e])
    return y


def setup_inputs(seed: int = 0) -> dict:
    key = jax.random.key(seed)
    ks = iter(jax.random.split(key, 40))

    def nrm(shape, scale):
        return jax.random.normal(next(ks), shape, jnp.float32) * scale

    a8 = jax.random.uniform(next(ks), (DEPTH, D_LRU), jnp.float32, 0.9, 0.999)
    a = a8 ** (1.0 / LRU_C)
    lam = jnp.log(a) - jnp.log1p(-a)
    return {
        'x_prompt': nrm((BATCH, SEQ, D_MODEL), 1.0),
        'x_sample': nrm((DEC_BATCH, DEC_SEQ, D_MODEL), 1.0),
        'cache_kv_latent': nrm((DEPTH, DEC_BATCH, PAST_LEN, KV_LORA), 1.0),
        'cache_k_rope': nrm((DEPTH, DEC_BATCH, PAST_LEN, ROPE_DIM), 1.0),
        'state_lru': nrm((DEPTH, DEC_BATCH, D_LRU), 0.5),
        'state_conv': nrm((DEPTH, DEC_BATCH, CONV_W - 1, D_LRU), 1.0),
        'w_in': nrm((DEPTH, D_MODEL, D_IN), D_MODEL ** -0.5),
        'q_norm_g': 1.0 + nrm((DEPTH, Q_LORA), 0.02),
        'w_uq': nrm((DEPTH, Q_LORA, N_HEADS * QK_DIM), Q_LORA ** -0.5),
        'kv_norm_g': 1.0 + nrm((DEPTH, KV_LORA), 0.02),
        'w_uk': nrm((DEPTH, KV_LORA, N_HEADS * NOPE_DIM), KV_LORA ** -0.5),
        'w_uv': nrm((DEPTH, KV_LORA, N_HEADS * V_DIM), KV_LORA ** -0.5 * DN_BETA),
        'conv_w': nrm((DEPTH, CONV_W, D_LRU), CONV_W ** -0.5),
        'conv_b': nrm((DEPTH, D_LRU), 0.02),
        'w_gate_a': nrm((DEPTH, LRU_BLOCKS, LRU_BW, LRU_BW), LRU_BW ** -0.5),
        'b_gate_a': nrm((DEPTH, LRU_BLOCKS, LRU_BW), 0.02),
        'w_gate_x': nrm((DEPTH, LRU_BLOCKS, LRU_BW, LRU_BW), LRU_BW ** -0.5),
        'b_gate_x': nrm((DEPTH, LRU_BLOCKS, LRU_BW), 0.02),
        'lru_lambda': lam,
        'w_o': nrm((DEPTH, D_MIX, D_MODEL), D_MIX ** -0.5 * DN_BETA),
        'ln1_g': 1.0 + nrm((DEPTH, D_MODEL), 0.02),
        'ln1_b': nrm((DEPTH, D_MODEL), 0.02),
        'ln2_g': 1.0 + nrm((DEPTH, D_MODEL), 0.02),
        'ln2_b': nrm((DEPTH, D_MODEL), 0.02),
        'ffn_w1': nrm((N_DENSE, D_MODEL, D_FF), D_MODEL ** -0.5),
        'ffn_w3': nrm((N_DENSE, D_MODEL, D_FF), D_MODEL ** -0.5),
        'ffn_w2': nrm((N_DENSE, D_FF, D_MODEL), D_FF ** -0.5 * DN_BETA),
        'router_w': nrm((N_MOE, D_MODEL, N_EXPERTS), D_MODEL ** -0.5),
        'moe_w1': nrm((N_MOE, N_EXPERTS, D_MODEL, D_FF_EXPERT), D_MODEL ** -0.5),
        'moe_w3': nrm((N_MOE, N_EXPERTS, D_MODEL, D_FF_EXPERT), D_MODEL ** -0.5),
        'moe_w2': nrm((N_MOE, N_EXPERTS, D_FF_EXPERT, D_MODEL), D_FF_EXPERT ** -0.5 * DN_BETA),
    }


def reference(x_prompt, x_sample, cache_kv_latent, cache_k_rope, state_lru, state_conv,
              w_in, q_norm_g, w_uq, kv_norm_g, w_uk, w_uv, conv_w, conv_b,
              w_gate_a, b_gate_a, w_gate_x, b_gate_x, lru_lambda, w_o,
              ln1_g, ln1_b, ln2_g, ln2_b, ffn_w1, ffn_w3, ffn_w2,
              router_w, moe_w1, moe_w3, moe_w2):
    mix_w = (w_in, q_norm_g, w_uq, kv_norm_g, w_uk, w_uv, conv_w, conv_b,
             w_gate_a, b_gate_a, w_gate_x, b_gate_x, lru_lambda, w_o)

    def trunk(x, pos, past_lat, past_kr, h0, conv0):
        B = x.shape[0]
        lats, krs, hs, bufs = [], [], [], []
        for l in range(DEPTH):
            if past_lat is None:
                pl, pk = None, None
                h_in = jnp.zeros((B, D_LRU), x.dtype)
                c_in = jnp.zeros((B, CONV_W - 1, D_LRU), x.dtype)
            else:
                pl, pk, h_in, c_in = past_lat[l], past_kr[l], h0[l], conv0[l]
            mix, lat, kr, h_last, buf = temporal_mix(x, pos, pl, pk, h_in, c_in, *[w[l] for w in mix_w])
            x = layer_norm(DN_ALPHA * x + mix, ln1_g[l], ln1_b[l])
            if l % 2 == 0:
                f = swiglu(x, ffn_w1[l // 2], ffn_w3[l // 2], ffn_w2[l // 2])
            else:
                f = moe(x, router_w[l // 2], moe_w1[l // 2], moe_w3[l // 2], moe_w2[l // 2])
            x = layer_norm(DN_ALPHA * x + f, ln2_g[l], ln2_b[l])
            lats.append(lat)
            krs.append(kr)
            hs.append(h_last)
            bufs.append(buf)
        return x, jnp.stack(lats), jnp.stack(krs), jnp.stack(hs), jnp.stack(bufs)

    pos_p = jnp.arange(x_prompt.shape[1], dtype=jnp.int32)
    pos_s = cache_kv_latent.shape[2] + jnp.arange(x_sample.shape[1], dtype=jnp.int32)
    y_prompt, p_lat, p_kr, p_lru, p_conv = trunk(x_prompt, pos_p, None, None, None, None)
    y_sample, s_lat, s_kr, s_lru, s_conv = trunk(x_sample, pos_s, cache_kv_latent, cache_k_rope, state_lru, state_conv)
    return (y_prompt, y_sample, p_lat, p_kr, p_lru, p_conv, s_lat, s_kr, s_lru, s_conv)
```

```python
import functools
import math

import jax
import jax.numpy as jnp
from jax import lax
from jax.experimental import pallas as pl
from jax.experimental.pallas import tpu as pltpu

CHUNK = 64
NOPE_DIM = 128
ROPE_DIM = 64
ROPE_HALF = ROPE_DIM // 2
V_DIM = 128
QK_DIM = NOPE_DIM + ROPE_DIM
ROPE_THETA = 10000.0
LRU_BW = 128
LRU_C = 8.0
TOP_K = 2
LN_EPS = 1e-5
RMS_EPS = 1e-6

LANES = 128
SUBLANES = 8
HEAD_PAD = 2 * LANES
V7X_VMEM_BYTES = 64 * 1024 * 1024
VMEM_LIMIT = V7X_VMEM_BYTES - 8 * 1024 * 1024

BF16 = jnp.bfloat16
F32 = jnp.float32


def _cparams(*sem):
    return pltpu.CompilerParams(dimension_semantics=sem, vmem_limit_bytes=VMEM_LIMIT)


def _tile(n, pref, mult=SUBLANES):
    if n <= pref:
        return n
    t = (pref // mult) * mult
    while t >= mult:
        if n % t == 0:
            return t
        t -= mult
    return n


def _layer_norm_rows(y, g, b):
    mu = jnp.mean(y, axis=-1, keepdims=True)
    yc = y - mu
    var = jnp.mean(yc * yc, axis=-1, keepdims=True)
    return yc * lax.rsqrt(var + LN_EPS) * g + b


def _rms_norm_rows(y, g):
    return y * lax.rsqrt(jnp.mean(y * y, axis=-1, keepdims=True) + RMS_EPS) * g


def _mm_cast_kernel(x_ref, w_ref, o_ref, xb_ref):
    @pl.when(pl.program_id(1) == 0)
    def _():
        xb_ref[...] = x_ref[...].astype(BF16)

    o_ref[...] = jnp.dot(xb_ref[...], w_ref[...], preferred_element_type=F32).astype(o_ref.dtype)


def _mm_kernel(x_ref, w_ref, o_ref):
    o_ref[...] = jnp.dot(x_ref[...], w_ref[...], preferred_element_type=F32).astype(o_ref.dtype)


def matmul(x, w, out_dtype, tm_pref=1024, tn_pref=1024):
    M, K = x.shape
    N = w.shape[1]
    tm = _tile(M, tm_pref)
    tn = _tile(N, tn_pref, LANES)
    cast = x.dtype != BF16
    return pl.pallas_call(
        _mm_cast_kernel if cast else _mm_kernel,
        grid=(M // tm, N // tn),
        in_specs=[pl.BlockSpec((tm, K), lambda i, j: (i, 0)),
                  pl.BlockSpec((K, tn), lambda i, j: (0, j))],
        out_specs=pl.BlockSpec((tm, tn), lambda i, j: (i, j)),
        out_shape=jax.ShapeDtypeStruct((M, N), out_dtype),
        scratch_shapes=[pltpu.VMEM((tm, K), BF16)] if cast else [],
        compiler_params=_cparams("parallel", "arbitrary"),
        name="matmul",
    )(x, w)


def _rope_table_kernel(inv_ref, cos_ref, sin_ref, *, pos0):
    rows = lax.broadcasted_iota(jnp.int32, cos_ref.shape, 0)
    lane = lax.broadcasted_iota(jnp.int32, cos_ref.shape, 1)
    ang = (rows + pos0).astype(F32) * inv_ref[...]
    live = lane < ROPE_DIM
    cos_ref[...] = jnp.where(live, jnp.cos(ang), 0.0)
    sin_ref[...] = jnp.where(live, jnp.sin(ang), 0.0)


def rope_tables(seq, pos0):
    inv = ROPE_THETA ** (-jnp.arange(ROPE_HALF, dtype=F32) / ROPE_HALF)
    inv128 = jnp.concatenate([inv, inv, jnp.zeros((LANES - ROPE_DIM,), F32)])[None, :]
    return pl.pallas_call(
        functools.partial(_rope_table_kernel, pos0=pos0),
        out_shape=(jax.ShapeDtypeStruct((seq, LANES), F32),) * 2,
        name="rope_tables",
    )(inv128)


def _latent_kernel(p_ref, cos_ref, sin_ref, qg_ref, kvg_ref, w1_ref, w2_ref, wkv_ref,
                   q_ref, kv_ref, lat_ref, kr_ref, krp_ref, *, q_lora, kv_lora, n_heads):
    cos = cos_ref[...]
    sin = sin_ref[...]
    cqn = _rms_norm_rows(p_ref[:, :q_lora], qg_ref[...]).astype(BF16)
    lat = _rms_norm_rows(p_ref[:, q_lora:q_lora + kv_lora], kvg_ref[...])
    lat_ref[...] = lat
    o = q_lora + kv_lora
    krope = p_ref[:, o:o + LANES] * cos + p_ref[:, o + LANES:o + 2 * LANES] * sin
    kr_ref[...] = krope[:, :ROPE_DIM]
    krp_ref[...] = krope.astype(BF16)
    kv_ref[...] = jnp.dot(lat.astype(BF16), wkv_ref[...], preferred_element_type=F32).astype(BF16)
    for h in range(n_heads):
        qa = jnp.dot(cqn, w1_ref[:, h * HEAD_PAD:(h + 1) * HEAD_PAD], preferred_element_type=F32)
        qb = jnp.dot(cqn, w2_ref[:, h * LANES:(h + 1) * LANES], preferred_element_type=F32)
        q_ref[:, h * HEAD_PAD:h * HEAD_PAD + LANES] = qa[:, :LANES].astype(BF16)
        q_ref[:, h * HEAD_PAD + LANES:(h + 1) * HEAD_PAD] = (qa[:, LANES:] * cos + qb * sin).astype(BF16)


def latent_stage(p, cos, sin, qg, kvg, w1, w2, wkv, n_heads):
    T = p.shape[0]
    q_lora, kv_lora = qg.shape[1], kvg.shape[1]
    tm = _tile(min(T, cos.shape[0]), 256)
    n_pos = cos.shape[0] // tm
    const = lambda i: (0, 0)
    row = lambda i: (i, 0)
    return pl.pallas_call(
        functools.partial(_latent_kernel, q_lora=q_lora, kv_lora=kv_lora, n_heads=n_heads),
        grid=(T // tm,),
        in_specs=[pl.BlockSpec((tm, p.shape[1]), row),
                  pl.BlockSpec((tm, LANES), lambda i: (i % n_pos, 0)),
                  pl.BlockSpec((tm, LANES), lambda i: (i % n_pos, 0)),
                  pl.BlockSpec(qg.shape, const), pl.BlockSpec(kvg.shape, const),
                  pl.BlockSpec(w1.shape, const), pl.BlockSpec(w2.shape, const),
                  pl.BlockSpec(wkv.shape, const)],
        out_specs=(pl.BlockSpec((tm, n_heads * HEAD_PAD), row),
                   pl.BlockSpec((tm, wkv.shape[1]), row),
                   pl.BlockSpec((tm, kv_lora), row),
                   pl.BlockSpec((tm, ROPE_DIM), row),
                   pl.BlockSpec((tm, LANES), row)),
        out_shape=(jax.ShapeDtypeStruct((T, n_heads * HEAD_PAD), BF16),
                   jax.ShapeDtypeStruct((T, wkv.shape[1]), BF16),
                   jax.ShapeDtypeStruct((T, kv_lora), F32),
                   jax.ShapeDtypeStruct((T, ROPE_DIM), F32),
                   jax.ShapeDtypeStruct((T, LANES), BF16)),
        compiler_params=_cparams("parallel"),
        name="latent_stage",
    )(p, cos, sin, qg, kvg, w1, w2, wkv)


def _attn_kernel(q_ref, kn_ref, kr_ref, v_ref, o_ref, m_sc, l_sc, acc_sc, *, tq, tk, sk, q_pos0):
    row0 = q_pos0 + pl.program_id(2) * tq
    first_vis = (row0 // CHUNK + 1) * CHUNK
    last_vis = ((row0 + tq - 1) // CHUNK + 1) * CHUNK
    kv_hi = jnp.minimum(last_vis, sk)
    n_full = sk // tk
    tail = sk - n_full * tk
    n_unmasked = jnp.minimum(first_vis, sk) // tk
    n_masked_end = jnp.minimum((kv_hi + tk - 1) // tk, n_full)
    q = q_ref[...]
    scale = QK_DIM ** -0.5

    m_sc[...] = jnp.full(m_sc.shape, -jnp.inf, F32)
    l_sc[...] = jnp.zeros(l_sc.shape, F32)
    acc_sc[...] = jnp.zeros(acc_sc.shape, F32)

    def step(start, size, masked):
        k = jnp.concatenate([kn_ref[pl.ds(start, size), :], kr_ref[pl.ds(start, size), :]], axis=1)
        s = lax.dot_general(q, k, (((1,), (1,)), ((), ())), preferred_element_type=F32) * scale
        if masked:
            qpos = row0 + lax.broadcasted_iota(jnp.int32, (tq, size), 0)
            kpos = start + lax.broadcasted_iota(jnp.int32, (tq, size), 1)
            s = jnp.where(kpos // CHUNK <= qpos // CHUNK, s, -jnp.inf)
        m_prev = m_sc[...]
        m_new = jnp.maximum(m_prev, jnp.max(s, axis=1, keepdims=True))
        alpha = jnp.exp(m_prev - m_new)
        p = jnp.exp(s - m_new)
        l_sc[...] = alpha * l_sc[...] + jnp.sum(p, axis=1, keepdims=True)
        acc_sc[...] = alpha * acc_sc[...] + jnp.dot(p.astype(BF16), v_ref[pl.ds(start, size), :],
                                                    preferred_element_type=F32)
        m_sc[...] = m_new

    def loop(lo, hi, masked):
        def body(j, c):
            step(pl.multiple_of(j * tk, tk), tk, masked)
            return c
        lax.fori_loop(lo, hi, body, 0)

    loop(0, n_unmasked, False)
    loop(n_unmasked, n_masked_end, True)
    if tail:
        @pl.when(kv_hi > n_full * tk)
        def _():
            step(n_full * tk, tail, True)

    o_ref[...] = acc_sc[...] / l_sc[...]


def attention(q, kv, krp, batch, n_heads, q_pos0):
    sq = q.shape[0] // batch
    sk = kv.shape[0] // batch
    tq = _tile(sq, 512)
    tk = min(512, sk)
    nq = sq // tq
    return pl.pallas_call(
        functools.partial(_attn_kernel, tq=tq, tk=tk, sk=sk, q_pos0=q_pos0),
        grid=(batch, n_heads, nq),
        in_specs=[pl.BlockSpec((tq, HEAD_PAD), lambda b, h, i: (b * nq + i, h)),
                  pl.BlockSpec((sk, NOPE_DIM), lambda b, h, i: (b, h)),
                  pl.BlockSpec((sk, LANES), lambda b, h, i: (b, 0)),
                  pl.BlockSpec((sk, V_DIM), lambda b, h, i: (b, n_heads + h))],
        out_specs=pl.BlockSpec((tq, V_DIM), lambda b, h, i: (b * nq + i, h)),
        out_shape=jax.ShapeDtypeStruct((batch * sq, n_heads * V_DIM), F32),
        scratch_shapes=[pltpu.VMEM((tq, 1), F32), pltpu.VMEM((tq, 1), F32),
                        pltpu.VMEM((tq, V_DIM), F32)],
        compiler_params=_cparams("parallel", "parallel", "arbitrary"),
        name="attention",
    )(q, kv, krp, kv)


def _lru_kernel(u_ref, gate_ref, ga_ref, h0_ref, conv0_ref, cw_ref, cb_ref, wa_ref, ba_ref,
                wx_ref, bx_ref, lam_ref, part_ref, hlast_ref, convlast_ref, ubuf, hc, *, ts, conv_w):
    pad = SUBLANES

    @pl.when(pl.program_id(2) == 0)
    def _():
        ubuf[0:pad, :] = conv0_ref[0]
        hc[...] = h0_ref[0]

    ubuf[pad:pad + ts, :] = u_ref[...]
    uc = cb_ref[...]
    for j in range(conv_w):
        k = conv_w - 1 - j
        uc = uc + ubuf[pad - k:pad - k + ts, :] * cw_ref[j:j + 1, :]
    convlast_ref[0] = ubuf[ts:ts + pad, :]
    ubuf[0:pad, :] = ubuf[ts:ts + pad, :]

    ucb = uc.astype(BF16)
    bc = uc.shape[1]
    r_parts, i_parts = [], []
    for n in range(bc // LRU_BW):
        blk = ucb[:, n * LRU_BW:(n + 1) * LRU_BW]
        r_parts.append(jnp.dot(blk, wa_ref[n], preferred_element_type=F32))
        i_parts.append(jnp.dot(blk, wx_ref[n], preferred_element_type=F32))
    r = jax.nn.sigmoid(jnp.concatenate(r_parts, axis=1) + ba_ref[...])
    ig = jax.nn.sigmoid(jnp.concatenate(i_parts, axis=1) + bx_ref[...])

    neg_lam = -lam_ref[...]
    softplus = jnp.maximum(neg_lam, 0.0) + jnp.log1p(jnp.exp(-jnp.abs(neg_lam)))
    log_a = (-LRU_C * r) * softplus
    a = jnp.exp(log_a)
    b = jnp.sqrt(-jnp.tanh(log_a) * (a * a + 1.0)) * (ig * uc)

    rows = lax.broadcasted_iota(jnp.int32, a.shape, 0)
    d = 1
    while d < ts:
        keep = rows >= d
        a_prev = jnp.where(keep, pltpu.roll(a, d, 0), 1.0)
        b_prev = jnp.where(keep, pltpu.roll(b, d, 0), 0.0)
        b = a * b_prev + b
        a = a * a_prev
        d *= 2
    h = b + a * hc[...]
    hc[...] = h[ts - 1:ts, :]
    hlast_ref[0] = h[ts - 1:ts, :]
    part_ref[...] = jax.nn.sigmoid(ga_ref[...]) * (h * jax.nn.gelu(gate_ref[...]))


def lru_branch(pb, h0, conv0p, cw, cb, wa, ba, wx, bx, lam, batch):
    T = pb.shape[0]
    W = pb.shape[1] // 4
    S = T // batch
    ts = _tile(S, 256)
    bc = min(W, 512)
    ns, nc = S // ts, W // bc
    nb = bc // LRU_BW
    conv_w = cw.shape[0]
    tile = lambda g: pl.BlockSpec((ts, bc), lambda b, c, s: (b * ns + s, g * nc + c))
    chan = lambda rows: pl.BlockSpec((rows, bc), lambda b, c, s: (0, c))
    state = lambda rows: pl.BlockSpec((1, rows, bc), lambda b, c, s: (b, 0, c))
    gatew = pl.BlockSpec((nb, LRU_BW, LRU_BW), lambda b, c, s: (c, 0, 0))
    return pl.pallas_call(
        functools.partial(_lru_kernel, ts=ts, conv_w=conv_w),
        grid=(batch, nc, ns),
        in_specs=[tile(0), tile(1), tile(2), state(1), state(SUBLANES), chan(conv_w), chan(1),
                  gatew, chan(1), gatew, chan(1), chan(1)],
        out_specs=(pl.BlockSpec((ts, bc), lambda b, c, s: (b * ns + s, c)), state(1), state(SUBLANES)),
        out_shape=(jax.ShapeDtypeStruct((T, W), F32),
                   jax.ShapeDtypeStruct((batch, 1, W), F32),
                   jax.ShapeDtypeStruct((batch, SUBLANES, W), F32)),
        scratch_shapes=[pltpu.VMEM((ts + SUBLANES, bc), F32), pltpu.VMEM((1, bc), F32)],
        compiler_params=_cparams("parallel", "parallel", "arbitrary"),
        name="lru_branch",
    )(pb, pb, pb, h0, conv0p, cw, cb, wa, ba, wx, bx, lam)


def _merge_out_kernel(part_ref, gb_ref, attn_ref, x_ref, wo_ref, g_ref, b_ref, y_ref, yb_ref, *, alpha):
    merged = part_ref[...] + jax.nn.sigmoid(gb_ref[...]) * attn_ref[...]
    out = jnp.dot(merged.astype(BF16), wo_ref[...], preferred_element_type=F32)
    y = _layer_norm_rows(alpha * x_ref[...] + out, g_ref[...], b_ref[...])
    y_ref[...] = y
    yb_ref[...] = y.astype(BF16)


def merge_out(part, pb, attn, x, wo, g, b, alpha):
    T, D = x.shape
    W = part.shape[1]
    tm = _tile(T, 256)
    row = lambda i: (i, 0)
    const = lambda i: (0, 0)
    return pl.pallas_call(
        functools.partial(_merge_out_kernel, alpha=alpha),
        grid=(T // tm,),
        in_specs=[pl.BlockSpec((tm, W), row), pl.BlockSpec((tm, W), lambda i: (i, 3)),
                  pl.BlockSpec((tm, W), row), pl.BlockSpec((tm, D), row),
                  pl.BlockSpec(wo.shape, const), pl.BlockSpec((1, D), const), pl.BlockSpec((1, D), const)],
        out_specs=(pl.BlockSpec((tm, D), row), pl.BlockSpec((tm, D), row)),
        out_shape=(jax.ShapeDtypeStruct((T, D), F32), jax.ShapeDtypeStruct((T, D), BF16)),
        compiler_params=_cparams("parallel"),
        name="merge_out",
    )(part, pb, attn, x, wo, g, b)


def _ffn_kernel(te_ref, nt_ref, x_ref, w1_ref, w3_ref, w2_ref, o_ref):
    del te_ref

    @pl.when(pl.program_id(0) < nt_ref[0])
    def _():
        x = x_ref[...]
        a = jnp.dot(x, w1_ref[0], preferred_element_type=F32)
        b = jnp.dot(x, w3_ref[0], preferred_element_type=F32)
        h = (a * jax.nn.sigmoid(a) * b).astype(BF16)
        y = jnp.dot(h, w2_ref[0], preferred_element_type=F32)

        @pl.when(pl.program_id(1) == 0)
        def _():
            o_ref[...] = y

        @pl.when(pl.program_id(1) != 0)
        def _():
            o_ref[...] += y

    @pl.when((pl.program_id(0) >= nt_ref[0]) & (pl.program_id(1) == 0))
    def _():
        o_ref[...] = jnp.zeros(o_ref.shape, o_ref.dtype)


def grouped_ffn(xs, tile_expert, n_tiles, w1, w3, w2, tm):
    P, D = xs.shape
    F = w1.shape[2]
    tf = _tile(F, 512, LANES)
    nf = F // tf
    def live(i, nt):
        return jnp.minimum(i, nt[0] - 1)
    def ff(i, f, nt):
        return jnp.where(i < nt[0], f, nf - 1)
    grid_spec = pltpu.PrefetchScalarGridSpec(
        num_scalar_prefetch=2,
        grid=(P // tm, nf),
        in_specs=[pl.BlockSpec((tm, D), lambda i, f, te, nt: (live(i, nt), 0)),
                  pl.BlockSpec((1, D, tf), lambda i, f, te, nt: (te[live(i, nt)], 0, ff(i, f, nt))),
                  pl.BlockSpec((1, D, tf), lambda i, f, te, nt: (te[live(i, nt)], 0, ff(i, f, nt))),
                  pl.BlockSpec((1, tf, D), lambda i, f, te, nt: (te[live(i, nt)], ff(i, f, nt), 0))],
        out_specs=pl.BlockSpec((tm, D), lambda i, f, te, nt: (i, 0)),
    )
    return pl.pallas_call(
        _ffn_kernel,
        grid_spec=grid_spec,
        out_shape=jax.ShapeDtypeStruct((P, D), F32),
        compiler_params=_cparams("parallel", "arbitrary"),
        name="grouped_ffn",
    )(tile_expert, n_tiles, xs, w1, w3, w2)


def _resid_ln_kernel(x_ref, f_ref, g_ref, b_ref, y_ref, yb_ref, *, alpha):
    y = _layer_norm_rows(alpha * x_ref[...] + f_ref[...], g_ref[...], b_ref[...])
    y_ref[...] = y
    yb_ref[...] = y.astype(BF16)


def resid_ln(x, f, g, b, alpha):
    T, D = x.shape
    tm = _tile(T, 512)
    row = lambda i: (i, 0)
    const = lambda i: (0, 0)
    return pl.pallas_call(
        functools.partial(_resid_ln_kernel, alpha=alpha),
        grid=(T // tm,),
        in_specs=[pl.BlockSpec((tm, D), row), pl.BlockSpec((tm, D), row),
                  pl.BlockSpec((1, D), const), pl.BlockSpec((1, D), const)],
        out_specs=(pl.BlockSpec((tm, D), row), pl.BlockSpec((tm, D), row)),
        out_shape=(jax.ShapeDtypeStruct((T, D), F32), jax.ShapeDtypeStruct((T, D), BF16)),
        compiler_params=_cparams("parallel"),
        name="resid_ln",
    )(x, f, g, b)


def _router_kernel(x_ref, w_ref, idx_ref, gate_ref, *, n_experts):
    logits = jnp.dot(x_ref[...], w_ref[...], preferred_element_type=F32, precision=lax.Precision.HIGHEST)
    lane = lax.broadcasted_iota(jnp.int32, logits.shape, 1)
    logits = jnp.where(lane < n_experts, logits, -jnp.inf)
    m1 = jnp.max(logits, axis=1, keepdims=True)
    i1 = jnp.min(jnp.where(logits == m1, lane, LANES), axis=1, keepdims=True)
    rest = jnp.where(lane == i1, -jnp.inf, logits)
    m2 = jnp.max(rest, axis=1, keepdims=True)
    i2 = jnp.min(jnp.where(rest == m2, lane, LANES), axis=1, keepdims=True)
    e2 = jnp.exp(m2 - m1)
    g1 = 1.0 / (1.0 + e2)
    g2 = e2 / (1.0 + e2)
    idx_ref[...] = jnp.where(lane == 0, i1, jnp.where(lane == 1, i2, 0))
    gate_ref[...] = jnp.where(lane == 0, g1, jnp.where(lane == 1, g2, 0.0))


def router(x, w_router):
    T, D = x.shape
    E = w_router.shape[1]
    wp = jnp.pad(w_router, ((0, 0), (0, LANES - E)))
    tm = _tile(T, 512)
    row = lambda i: (i, 0)
    return pl.pallas_call(
        functools.partial(_router_kernel, n_experts=E),
        grid=(T // tm,),
        in_specs=[pl.BlockSpec((tm, D), row), pl.BlockSpec((D, LANES), lambda i: (0, 0))],
        out_specs=(pl.BlockSpec((tm, LANES), row), pl.BlockSpec((tm, LANES), row)),
        out_shape=(jax.ShapeDtypeStruct((T, LANES), jnp.int32), jax.ShapeDtypeStruct((T, LANES), F32)),
        compiler_params=_cparams("parallel"),
        name="router",
    )(x, wp)


def _gather_kernel(tok_ref, x_hbm, o_ref, buf, sem, *, tg):
    base = pl.program_id(0) * tg

    def issue(r, c):
        pltpu.make_async_copy(x_hbm.at[pl.ds(tok_ref[base + r], 1)], buf.at[pl.ds(r, 1)], sem).start()
        return c

    lax.fori_loop(0, tg, issue, 0)
    pltpu.make_async_copy(x_hbm.at[pl.ds(0, tg)], buf, sem).wait()
    o_ref[...] = buf[...].astype(BF16)


def gather_rows(x, row_token, tg):
    P = row_token.shape[0]
    D = x.shape[1]
    grid_spec = pltpu.PrefetchScalarGridSpec(
        num_scalar_prefetch=1,
        grid=(P // tg,),
        in_specs=[pl.BlockSpec(memory_space=pl.ANY)],
        out_specs=pl.BlockSpec((tg, D), lambda i, tok: (i, 0)),
        scratch_shapes=[pltpu.VMEM((tg, D), F32), pltpu.SemaphoreType.DMA],
    )
    return pl.pallas_call(
        functools.partial(_gather_kernel, tg=tg),
        grid_spec=grid_spec,
        out_shape=jax.ShapeDtypeStruct((P, D), BF16),
        compiler_params=_cparams("arbitrary"),
        name="gather_rows",
    )(row_token, x)


def _combine_kernel(pos_ref, ys_hbm, gate_ref, x_ref, g_ref, b_ref, y_ref, yb_ref, buf0, buf1, sem0, sem1,
                    *, tc, alpha):
    base = pl.program_id(0) * tc

    def issue(r, c):
        p = (base + r) * TOP_K
        pltpu.make_async_copy(ys_hbm.at[pl.ds(pos_ref[p], 1)], buf0.at[pl.ds(r, 1)], sem0).start()
        pltpu.make_async_copy(ys_hbm.at[pl.ds(pos_ref[p + 1], 1)], buf1.at[pl.ds(r, 1)], sem1).start()
        return c

    lax.fori_loop(0, tc, issue, 0)
    pltpu.make_async_copy(ys_hbm.at[pl.ds(0, tc)], buf0, sem0).wait()
    pltpu.make_async_copy(ys_hbm.at[pl.ds(0, tc)], buf1, sem1).wait()
    gates = gate_ref[...]
    f = gates[:, 0:1] * buf0[...] + gates[:, 1:2] * buf1[...]
    y = _layer_norm_rows(alpha * x_ref[...] + f, g_ref[...], b_ref[...])
    y_ref[...] = y
    yb_ref[...] = y.astype(BF16)


def combine_ln(ys, pos, gates, x, g, b, alpha):
    T, D = x.shape
    tc = _tile(T, 256)
    row = lambda i, pos: (i, 0)
    const = lambda i, pos: (0, 0)
    grid_spec = pltpu.PrefetchScalarGridSpec(
        num_scalar_prefetch=1,
        grid=(T // tc,),
        in_specs=[pl.BlockSpec(memory_space=pl.ANY), pl.BlockSpec((tc, LANES), row),
                  pl.BlockSpec((tc, D), row), pl.BlockSpec((1, D), const), pl.BlockSpec((1, D), const)],
        out_specs=(pl.BlockSpec((tc, D), row), pl.BlockSpec((tc, D), row)),
        scratch_shapes=[pltpu.VMEM((tc, D), F32), pltpu.VMEM((tc, D), F32),
                        pltpu.SemaphoreType.DMA, pltpu.SemaphoreType.DMA],
    )
    return pl.pallas_call(
        functools.partial(_combine_kernel, tc=tc, alpha=alpha),
        grid_spec=grid_spec,
        out_shape=(jax.ShapeDtypeStruct((T, D), F32), jax.ShapeDtypeStruct((T, D), BF16)),
        compiler_params=_cparams("arbitrary"),
        name="combine_ln",
    )(pos, ys, gates, x, g, b)


def moe_block(x, xb_unused, w_router, w1, w3, w2, g, b, alpha):
    del xb_unused
    T, D = x.shape
    E = w1.shape[0]
    tm = 1024 if T * TOP_K >= 8 * 1024 else LANES
    idx, gates = router(x, w_router)
    e_flat = idx[:, :TOP_K].reshape(-1)
    onehot = (e_flat[:, None] == jnp.arange(E, dtype=jnp.int32)[None, :]).astype(jnp.int32)
    rank = jnp.take_along_axis(jnp.cumsum(onehot, axis=0) - onehot, e_flat[:, None], axis=1)[:, 0]
    counts = jnp.sum(onehot, axis=0)
    padded = ((counts + tm - 1) // tm) * tm
    ends = jnp.cumsum(padded)
    pos = (ends - padded)[e_flat] + rank
    P = T * TOP_K + E * tm
    row_token = jnp.zeros((P,), jnp.int32).at[pos].set(jnp.arange(T * TOP_K, dtype=jnp.int32) // TOP_K)
    tile_start = jnp.arange(P // tm, dtype=jnp.int32) * tm
    tile_expert = jnp.minimum(jnp.searchsorted(ends, tile_start, side="right"), E - 1).astype(jnp.int32)
    n_tiles = (ends[-1:] // tm).astype(jnp.int32)

    xs = gather_rows(x, row_token, _tile(tm, 512))
    ys = grouped_ffn(xs, tile_expert, n_tiles, w1, w3, w2, tm)
    return combine_ln(ys, pos.astype(jnp.int32), gates, x, g, b, alpha)


def dense_block(x, xb, w1, w3, w2, g, b, alpha):
    T = x.shape[0]
    tm = _tile(T, 1024)
    f = grouped_ffn(xb, jnp.zeros((T // tm,), jnp.int32), jnp.full((1,), T // tm, jnp.int32), w1, w3, w2, tm)
    return resid_ln(x, f, g, b, alpha)


def _rot_cols(w):
    return jnp.concatenate([-w[..., ROPE_HALF:], w[..., :ROPE_HALF]], axis=-1)


def _prep_layer(l, w_in, q_norm_g, w_uq, kv_norm_g, w_uk, w_uv, conv_w, conv_b, w_gate_a, b_gate_a,
                w_gate_x, b_gate_x, lru_lambda, w_o):
    D = w_in.shape[1]
    q_lora, kv_lora = q_norm_g.shape[1], kv_norm_g.shape[1]
    n_heads = w_uk.shape[2] // NOPE_DIM
    W = conv_w.shape[2]
    wi = w_in[l]
    o = q_lora + kv_lora
    kr = wi[:, o:o + ROPE_DIM]
    z = jnp.zeros((D, LANES - ROPE_DIM), F32)
    w_lat = jnp.concatenate([wi[:, :o], kr, z, _rot_cols(kr), z], axis=1).astype(BF16)
    w_b = wi[:, o + ROPE_DIM:].astype(BF16)
    uq = w_uq[l].reshape(q_lora, n_heads, QK_DIM)
    nope, ropew = uq[..., :NOPE_DIM], uq[..., NOPE_DIM:]
    zq = jnp.zeros((q_lora, n_heads, LANES - ROPE_DIM), F32)
    w1 = jnp.concatenate([nope, ropew, zq], axis=-1).reshape(q_lora, n_heads * HEAD_PAD).astype(BF16)
    w2 = jnp.concatenate([_rot_cols(ropew), zq], axis=-1).reshape(q_lora, n_heads * LANES).astype(BF16)
    wkv = jnp.concatenate([w_uk[l], w_uv[l]], axis=1).astype(BF16)
    return dict(
        w_lat=w_lat, w_b=w_b, qg=q_norm_g[l][None, :], kvg=kv_norm_g[l][None, :], w1=w1, w2=w2, wkv=wkv,
        n_heads=n_heads, cw=conv_w[l], cb=conv_b[l][None, :],
        wa=w_gate_a[l].astype(BF16), ba=b_gate_a[l].reshape(1, W),
        wx=w_gate_x[l].astype(BF16), bx=b_gate_x[l].reshape(1, W),
        lam=lru_lambda[l][None, :], wo=w_o[l].astype(BF16))


def _pad_ff(w1, w3, w2):
    F = w1.shape[-1]
    Fp = -(-F // 512) * 512
    if Fp != F:
        w1 = jnp.pad(w1, ((0, 0), (0, 0), (0, Fp - F)))
        w3 = jnp.pad(w3, ((0, 0), (0, 0), (0, Fp - F)))
        w2 = jnp.pad(w2, ((0, 0), (0, Fp - F), (0, 0)))
    return w1.astype(BF16), w3.astype(BF16), w2.astype(BF16)


def kernel(x_prompt, x_sample, cache_kv_latent, cache_k_rope, state_lru, state_conv, w_in, q_norm_g, w_uq, kv_norm_g, w_uk, w_uv, conv_w, conv_b, w_gate_a, b_gate_a, w_gate_x, b_gate_x, lru_lambda, w_o, ln1_g, ln1_b, ln2_g, ln2_b, ffn_w1, ffn_w3, ffn_w2, router_w, moe_w1, moe_w3, moe_w2):
    depth = w_in.shape[0]
    alpha = (2 * depth) ** 0.25
    B, S, D = x_prompt.shape
    Bs, Ss, _ = x_sample.shape
    past = cache_kv_latent.shape[2]
    W = conv_w.shape[2]
    conv_pad = SUBLANES - (conv_w.shape[1] - 1)

    layers = [_prep_layer(l, w_in, q_norm_g, w_uq, kv_norm_g, w_uk, w_uv, conv_w, conv_b, w_gate_a, b_gate_a,
                          w_gate_x, b_gate_x, lru_lambda, w_o) for l in range(depth)]
    ffn = []
    for l in range(depth):
        if l % 2 == 0:
            ffn.append(_pad_ff(ffn_w1[l // 2][None], ffn_w3[l // 2][None], ffn_w2[l // 2][None]))
        else:
            ffn.append(_pad_ff(moe_w1[l // 2], moe_w3[l // 2], moe_w2[l // 2]))

    cos_p, sin_p = rope_tables(S, 0)
    cos_s, sin_s = rope_tables(Ss, past)
    cos_s, sin_s = jnp.tile(cos_s, (Bs, 1)), jnp.tile(sin_s, (Bs, 1))

    def trunk(x3, cos, sin, pos0, past_lat, past_kr, h0, conv0):
        Bg, Sg, _ = x3.shape
        x = x3.reshape(Bg * Sg, D)
        xb = x
        lats, krs, hs, bufs = [], [], [], []
        for l in range(depth):
            lw = layers[l]
            H = lw["n_heads"]
            p_lat = matmul(xb, lw["w_lat"], F32, tn_pref=lw["w_lat"].shape[1])
            pb = matmul(xb, lw["w_b"], F32)
            q, kv, lat, kr, krp = latent_stage(p_lat, cos, sin, lw["qg"], lw["kvg"], lw["w1"], lw["w2"],
                                               lw["wkv"], H)
            if past_lat is None:
                kv_all, krp_all = kv, krp
                h_in = jnp.zeros((Bg, 1, W), F32)
                c_in = jnp.zeros((Bg, SUBLANES, W), F32)
            else:
                n_past = past_lat.shape[2]
                kv_past = matmul(past_lat[l].reshape(Bg * n_past, -1), lw["wkv"], BF16)
                kv_all = jnp.concatenate([kv_past.reshape(Bg, n_past, -1), kv.reshape(Bg, Sg, -1)], axis=1)
                kv_all = kv_all.reshape(Bg * (n_past + Sg), -1)
                krp_past = jnp.pad(past_kr[l], ((0, 0), (0, 0), (0, LANES - ROPE_DIM))).astype(BF16)
                krp_all = jnp.concatenate([krp_past, krp.reshape(Bg, Sg, LANES)], axis=1)
                krp_all = krp_all.reshape(Bg * (n_past + Sg), LANES)
                h_in = h0[l][:, None, :]
                c_in = jnp.pad(conv0[l], ((0, 0), (conv_pad, 0), (0, 0)))
            attn = attention(q, kv_all, krp_all, Bg, H, pos0)
            part, h_last, conv_last = lru_branch(pb, h_in, c_in, lw["cw"], lw["cb"], lw["wa"], lw["ba"],
                                                 lw["wx"], lw["bx"], lw["lam"], Bg)
            x, xb = merge_out(part, pb, attn, x, lw["wo"], ln1_g[l][None, :], ln1_b[l][None, :], alpha)
            w1, w3, w2 = ffn[l]
            if l % 2 == 0:
                x, xb = dense_block(x, xb, w1, w3, w2, ln2_g[l][None, :], ln2_b[l][None, :], alpha)
            else:
                x, xb = moe_block(x, xb, router_w[l // 2], w1, w3, w2, ln2_g[l][None, :], ln2_b[l][None, :],
                                  alpha)
            lats.append(lat.reshape(Bg, Sg, -1))
            krs.append(kr.reshape(Bg, Sg, ROPE_DIM))
            hs.append(h_last[:, 0, :])
            bufs.append(conv_last[:, conv_pad:, :])
        return x.reshape(Bg, Sg, D), jnp.stack(lats), jnp.stack(krs), jnp.stack(hs), jnp.stack(bufs)

    y_p, p_lat, p_kr, p_lru, p_conv = trunk(x_prompt, cos_p, sin_p, 0, None, None, None, None)
    y_s, s_lat, s_kr, s_lru, s_conv = trunk(x_sample, cos_s, sin_s, past, cache_kv_latent, cache_k_rope,
                                            state_lru, state_conv)
    return (y_p, y_s, p_lat, p_kr, p_lru, p_conv, s_lat, s_kr, s_lru, s_conv)
```

```python
import functools
import math

import jax
import jax.numpy as jnp
from jax import lax
from jax.experimental import pallas as pl
from jax.experimental.pallas import tpu as pltpu

CHUNK = 64
NOPE_DIM = 128
ROPE_DIM = 64
ROPE_HALF = ROPE_DIM // 2
V_DIM = 128
QK_DIM = NOPE_DIM + ROPE_DIM
SCORE_SCALE = QK_DIM ** -0.5 * math.log2(math.e)
ROPE_THETA = 10000.0
LRU_BW = 128
LRU_C = 8.0
TOP_K = 2
LN_EPS = 1e-5
RMS_EPS = 1e-6

LANES = 128
SUBLANES = 8
HEAD_PAD = 2 * LANES
V7X_VMEM_BYTES = 64 * 1024 * 1024
VMEM_LIMIT = V7X_VMEM_BYTES - 8 * 1024 * 1024

BF16 = jnp.bfloat16
F32 = jnp.float32


def _cparams(*sem):
    return pltpu.CompilerParams(dimension_semantics=sem, vmem_limit_bytes=VMEM_LIMIT)


def _tile(n, pref, mult=SUBLANES):
    if n <= pref:
        return n
    t = (pref // mult) * mult
    while t >= mult:
        if n % t == 0:
            return t
        t -= mult
    return n


def _layer_norm_rows(y, g, b):
    mu = jnp.mean(y, axis=-1, keepdims=True)
    yc = y - mu
    var = jnp.mean(yc * yc, axis=-1, keepdims=True)
    return yc * lax.rsqrt(var + LN_EPS) * g + b


def _rms_norm_rows(y, g):
    return y * lax.rsqrt(jnp.mean(y * y, axis=-1, keepdims=True) + RMS_EPS) * g


def _mm_cast_kernel(x_ref, w_ref, o_ref, xb_ref):
    @pl.when(pl.program_id(1) == 0)
    def _():
        xb_ref[...] = x_ref[...].astype(BF16)

    o_ref[...] = jnp.dot(xb_ref[...], w_ref[...], preferred_element_type=F32).astype(o_ref.dtype)


def _mm_kernel(x_ref, w_ref, o_ref):
    o_ref[...] = jnp.dot(x_ref[...], w_ref[...], preferred_element_type=F32).astype(o_ref.dtype)


def matmul(x, w, out_dtype, tm_pref=1024, tn_pref=1024):
    M, K = x.shape
    N = w.shape[1]
    tm = _tile(M, tm_pref)
    tn = _tile(N, tn_pref, LANES)
    cast = x.dtype != BF16
    return pl.pallas_call(
        _mm_cast_kernel if cast else _mm_kernel,
        grid=(M // tm, N // tn),
        in_specs=[pl.BlockSpec((tm, K), lambda i, j: (i, 0)),
                  pl.BlockSpec((K, tn), lambda i, j: (0, j))],
        out_specs=pl.BlockSpec((tm, tn), lambda i, j: (i, j)),
        out_shape=jax.ShapeDtypeStruct((M, N), out_dtype),
        scratch_shapes=[pltpu.VMEM((tm, K), BF16)] if cast else [],
        compiler_params=_cparams("parallel", "arbitrary"),
        name="matmul",
    )(x, w)


def _rope_table_kernel(inv_ref, cos_ref, sin_ref, *, pos0):
    rows = lax.broadcasted_iota(jnp.int32, cos_ref.shape, 0)
    lane = lax.broadcasted_iota(jnp.int32, cos_ref.shape, 1)
    ang = (rows + pos0).astype(F32) * inv_ref[...]
    live = lane < ROPE_DIM
    cos_ref[...] = jnp.where(live, jnp.cos(ang), 0.0)
    sin_ref[...] = jnp.where(live, jnp.sin(ang), 0.0)


def rope_tables(seq, pos0):
    inv = ROPE_THETA ** (-jnp.arange(ROPE_HALF, dtype=F32) / ROPE_HALF)
    inv128 = jnp.concatenate([inv, inv, jnp.zeros((LANES - ROPE_DIM,), F32)])[None, :]
    return pl.pallas_call(
        functools.partial(_rope_table_kernel, pos0=pos0),
        out_shape=(jax.ShapeDtypeStruct((seq, LANES), F32),) * 2,
        name="rope_tables",
    )(inv128)


def _latent_kernel(p_ref, cos_ref, sin_ref, qg_ref, kvg_ref, w1_ref, w2_ref, wkv_ref,
                   q_ref, kc_ref, ve_ref, lat_ref, kr_ref, *, q_lora, kv_lora, n_heads):
    cos = cos_ref[...]
    sin = sin_ref[...]
    cqn = _rms_norm_rows(p_ref[:, :q_lora], qg_ref[...]).astype(BF16)
    lat = _rms_norm_rows(p_ref[:, q_lora:q_lora + kv_lora], kvg_ref[...])
    lat_ref[...] = lat
    o = q_lora + kv_lora
    krope = p_ref[:, o:o + LANES] * cos + p_ref[:, o + LANES:o + 2 * LANES] * sin
    kr_ref[...] = krope[:, :ROPE_DIM]
    krope_b = krope.astype(BF16)
    ones = jnp.ones(krope_b.shape, BF16)
    kv = jnp.dot(lat.astype(BF16), wkv_ref[...], preferred_element_type=F32).astype(BF16)
    for h in range(n_heads):
        lo, hi = h * HEAD_PAD, h * HEAD_PAD + LANES
        qa = jnp.dot(cqn, w1_ref[:, lo:lo + HEAD_PAD], preferred_element_type=F32)
        qb = jnp.dot(cqn, w2_ref[:, h * LANES:(h + 1) * LANES], preferred_element_type=F32)
        q_ref[:, lo:hi] = (qa[:, :LANES] * SCORE_SCALE).astype(BF16)
        q_ref[:, hi:hi + LANES] = ((qa[:, LANES:] * cos + qb * sin) * SCORE_SCALE).astype(BF16)
        kc_ref[:, lo:hi] = kv[:, h * NOPE_DIM:(h + 1) * NOPE_DIM]
        kc_ref[:, hi:hi + LANES] = krope_b
        ve_ref[:, lo:hi] = kv[:, (n_heads + h) * V_DIM:(n_heads + h + 1) * V_DIM]
        ve_ref[:, hi:hi + LANES] = ones


def latent_stage(p, cos, sin, qg, kvg, w1, w2, wkv, n_heads):
    T = p.shape[0]
    q_lora, kv_lora = qg.shape[1], kvg.shape[1]
    tm = _tile(min(T, cos.shape[0]), 256)
    n_pos = cos.shape[0] // tm
    const = lambda i: (0, 0)
    row = lambda i: (i, 0)
    head_cols = n_heads * HEAD_PAD
    return pl.pallas_call(
        functools.partial(_latent_kernel, q_lora=q_lora, kv_lora=kv_lora, n_heads=n_heads),
        grid=(T // tm,),
        in_specs=[pl.BlockSpec((tm, p.shape[1]), row),
                  pl.BlockSpec((tm, LANES), lambda i: (i % n_pos, 0)),
                  pl.BlockSpec((tm, LANES), lambda i: (i % n_pos, 0)),
                  pl.BlockSpec(qg.shape, const), pl.BlockSpec(kvg.shape, const),
                  pl.BlockSpec(w1.shape, const), pl.BlockSpec(w2.shape, const),
                  pl.BlockSpec(wkv.shape, const)],
        out_specs=(pl.BlockSpec((tm, head_cols), row),
                   pl.BlockSpec((tm, head_cols), row),
                   pl.BlockSpec((tm, head_cols), row),
                   pl.BlockSpec((tm, kv_lora), row),
                   pl.BlockSpec((tm, ROPE_DIM), row)),
        out_shape=(jax.ShapeDtypeStruct((T, head_cols), BF16),
                   jax.ShapeDtypeStruct((T, head_cols), BF16),
                   jax.ShapeDtypeStruct((T, head_cols), BF16),
                   jax.ShapeDtypeStruct((T, kv_lora), F32),
                   jax.ShapeDtypeStruct((T, ROPE_DIM), F32)),
        compiler_params=_cparams("parallel"),
        name="latent_stage",
    )(p, cos, sin, qg, kvg, w1, w2, wkv)


ATTN_TK = 512
ATTN_SM_ROWS = 128
ATTN_PV_ROWS = 256


def _chunk_mask(s, q0, k0):
    qpos = q0 + lax.broadcasted_iota(jnp.int32, s.shape, 0)
    kpos = k0 + lax.broadcasted_iota(jnp.int32, s.shape, 1)
    return jnp.where(kpos // CHUNK <= qpos // CHUNK, s, -jnp.inf)


def _attn_out(acc_sc):
    return acc_sc[:, :V_DIM] / acc_sc[:, V_DIM:]


def _attn_aligned_kernel(q_ref, k_ref, v_ref, o_ref, m_sc, acc_sc, s0, s1, p_sc, *, tq, tk, q_pos0):
    row0 = q_pos0 + pl.program_id(2) * tq
    n_pairs = row0 // (2 * tk)
    n_groups = tq // ATTN_PV_ROWS
    lower = tuple(range(n_groups // 2))
    upper = tuple(range(n_groups // 2, n_groups))

    m_sc[...] = jnp.full(m_sc.shape, -jnp.inf, F32)
    acc_sc[...] = jnp.zeros(acc_sc.shape, F32)

    def qk(start, s_ref, rows=slice(None)):
        s_ref[rows, :] = lax.dot_general(q_ref[rows, :], k_ref[pl.ds(start, tk), :], (((1,), (1,)), ((), ())),
                                         preferred_element_type=F32)

    def sm_pv(start, s_ref, groups, masked):
        for g in groups:
            for r in range(ATTN_PV_ROWS // ATTN_SM_ROWS):
                r0 = g * ATTN_PV_ROWS + r * ATTN_SM_ROWS
                rs = slice(r0, r0 + ATTN_SM_ROWS)
                s = s_ref[rs, :]
                if masked:
                    s = _chunk_mask(s, r0 % tk, 0)
                m_prev = m_sc[rs, :]
                m_new = jnp.maximum(m_prev, jnp.max(s, axis=1, keepdims=True))
                p_sc[rs, :] = jnp.exp2(s - m_new[:, :1]).astype(BF16)
                alpha = jnp.exp2(m_prev - m_new)
                acc_sc[rs, :] = jnp.concatenate([alpha, alpha], axis=1) * acc_sc[rs, :]
                m_sc[rs, :] = m_new
            gs = slice(g * ATTN_PV_ROWS, (g + 1) * ATTN_PV_ROWS)
            acc_sc[gs, :] += jnp.dot(p_sc[gs, :], v_ref[pl.ds(start, tk), :], preferred_element_type=F32)

    qk(0, s0)

    def body(t, c):
        a = pl.multiple_of(2 * t * tk, 2 * tk)
        qk(a + tk, s1)
        sm_pv(a, s0, lower + upper, False)
        qk(a + 2 * tk, s0)
        sm_pv(a + tk, s1, lower + upper, False)
        return c

    lax.fori_loop(0, n_pairs, body, 0)
    d0 = pl.multiple_of(2 * n_pairs * tk, 2 * tk)
    qk(d0 + tk, s1, slice(tq // 2, tq))
    sm_pv(d0, s0, lower, True)
    sm_pv(d0, s0, upper, False)
    sm_pv(d0 + tk, s1, upper, True)
    o_ref[...] = _attn_out(acc_sc)


def _attn_general_kernel(q_ref, k_ref, v_ref, o_ref, m_sc, acc_sc, *, tq, tk, sk, q_pos0):
    row0 = q_pos0 + pl.program_id(2) * tq
    first_vis = (row0 // CHUNK + 1) * CHUNK
    last_vis = ((row0 + tq - 1) // CHUNK + 1) * CHUNK
    kv_hi = jnp.minimum(last_vis, sk)
    n_full = sk // tk
    tail = sk - n_full * tk
    n_unmasked = jnp.minimum(first_vis, sk) // tk
    n_masked_end = jnp.minimum((kv_hi + tk - 1) // tk, n_full)

    m_sc[...] = jnp.full(m_sc.shape, -jnp.inf, F32)
    acc_sc[...] = jnp.zeros(acc_sc.shape, F32)

    def step(start, size, masked):
        s = lax.dot_general(q_ref[...], k_ref[pl.ds(start, size), :], (((1,), (1,)), ((), ())),
                            preferred_element_type=F32)
        if masked:
            s = _chunk_mask(s, row0, start)
        m_prev = m_sc[...]
        m_new = jnp.maximum(m_prev, jnp.max(s, axis=1, keepdims=True))
        p = jnp.exp2(s - m_new[:, :1]).astype(BF16)
        alpha = jnp.exp2(m_prev - m_new)
        acc_sc[...] = (jnp.concatenate([alpha, alpha], axis=1) * acc_sc[...]
                       + jnp.dot(p, v_ref[pl.ds(start, size), :], preferred_element_type=F32))
        m_sc[...] = m_new

    def loop(lo, hi, masked):
        def body(j, c):
            step(pl.multiple_of(j * tk, tk), tk, masked)
            return c
        lax.fori_loop(lo, hi, body, 0)

    loop(0, n_unmasked, False)
    loop(n_unmasked, n_masked_end, True)
    if tail:
        @pl.when(kv_hi > n_full * tk)
        def _():
            step(n_full * tk, tail, True)

    o_ref[...] = _attn_out(acc_sc)


def attention(q, kc, ve, batch, n_heads, q_pos0):
    sq = q.shape[0] // batch
    sk = kc.shape[0] // batch
    tk = min(ATTN_TK, sk)
    aligned = sq % (2 * tk) == 0 and q_pos0 % (2 * tk) == 0 and sk == q_pos0 + sq
    tq = 2 * tk if aligned else _tile(sq, ATTN_TK)
    nq = sq // tq
    scratch = [pltpu.VMEM((tq, LANES), F32), pltpu.VMEM((tq, 2 * V_DIM), F32)]
    if aligned:
        body = functools.partial(_attn_aligned_kernel, tq=tq, tk=tk, q_pos0=q_pos0)
        scratch += [pltpu.VMEM((tq, tk), F32), pltpu.VMEM((tq, tk), F32), pltpu.VMEM((tq, tk), BF16)]
    else:
        body = functools.partial(_attn_general_kernel, tq=tq, tk=tk, sk=sk, q_pos0=q_pos0)
    return pl.pallas_call(
        body,
        grid=(batch, n_heads, nq),
        in_specs=[pl.BlockSpec((tq, HEAD_PAD), lambda b, h, i: (b * nq + i, h)),
                  pl.BlockSpec((sk, HEAD_PAD), lambda b, h, i: (b, h)),
                  pl.BlockSpec((sk, 2 * V_DIM), lambda b, h, i: (b, h))],
        out_specs=pl.BlockSpec((tq, V_DIM), lambda b, h, i: (b * nq + i, h)),
        out_shape=jax.ShapeDtypeStruct((batch * sq, n_heads * V_DIM), F32),
        scratch_shapes=scratch,
        compiler_params=_cparams("parallel", "parallel", "arbitrary"),
        name="attention",
    )(q, kc, ve)


def _lru_kernel(u_ref, gate_ref, ga_ref, h0_ref, conv0_ref, cw_ref, cb_ref, wa_ref, ba_ref,
                wx_ref, bx_ref, lam_ref, part_ref, hlast_ref, convlast_ref, ubuf, hc, *, ts, conv_w):
    pad = SUBLANES

    @pl.when(pl.program_id(2) == 0)
    def _():
        ubuf[0:pad, :] = conv0_ref[0]
        hc[...] = h0_ref[0]

    ubuf[pad:pad + ts, :] = u_ref[...]
    uc = cb_ref[...]
    for j in range(conv_w):
        k = conv_w - 1 - j
        uc = uc + ubuf[pad - k:pad - k + ts, :] * cw_ref[j:j + 1, :]
    convlast_ref[0] = ubuf[ts:ts + pad, :]
    ubuf[0:pad, :] = ubuf[ts:ts + pad, :]

    ucb = uc.astype(BF16)
    bc = uc.shape[1]
    r_parts, i_parts = [], []
    for n in range(bc // LRU_BW):
        blk = ucb[:, n * LRU_BW:(n + 1) * LRU_BW]
        r_parts.append(jnp.dot(blk, wa_ref[n], preferred_element_type=F32))
        i_parts.append(jnp.dot(blk, wx_ref[n], preferred_element_type=F32))
    r = jax.nn.sigmoid(jnp.concatenate(r_parts, axis=1) + ba_ref[...])
    ig = jax.nn.sigmoid(jnp.concatenate(i_parts, axis=1) + bx_ref[...])

    neg_lam = -lam_ref[...]
    softplus = jnp.maximum(neg_lam, 0.0) + jnp.log1p(jnp.exp(-jnp.abs(neg_lam)))
    log_a = (-LRU_C * r) * softplus
    a = jnp.exp(log_a)
    z = -jnp.tanh(log_a) * (a * a + 1.0)
    b = jnp.where(z > 0.0, z * lax.rsqrt(z), 0.0) * (ig * uc)

    row_in_group = lax.broadcasted_iota(jnp.int32, a.shape, 0) % SUBLANES
    d = 1
    while d < SUBLANES:
        keep = row_in_group >= d
        a_prev = jnp.where(keep, pltpu.roll(a, d, 0), 1.0)
        b_prev = jnp.where(keep, pltpu.roll(b, d, 0), 0.0)
        b = a * b_prev + b
        a = a * a_prev
        d *= 2
    carry = hc[...]
    groups = []
    for g in range(ts // SUBLANES):
        rows = slice(g * SUBLANES, (g + 1) * SUBLANES)
        hg = b[rows, :] + a[rows, :] * carry
        carry = hg[SUBLANES - 1:SUBLANES, :]
        groups.append(hg)
    h = jnp.concatenate(groups, axis=0)
    hc[...] = carry
    hlast_ref[0] = carry
    part_ref[...] = jax.nn.sigmoid(ga_ref[...]) * (h * jax.nn.gelu(gate_ref[...]))


def lru_branch(pb, h0, conv0p, cw, cb, wa, ba, wx, bx, lam, batch):
    T = pb.shape[0]
    W = pb.shape[1] // 4
    S = T // batch
    ts = _tile(S, 256)
    bc = min(W, 512)
    ns, nc = S // ts, W // bc
    nb = bc // LRU_BW
    conv_w = cw.shape[0]
    tile = lambda g: pl.BlockSpec((ts, bc), lambda b, c, s: (b * ns + s, g * nc + c))
    chan = lambda rows: pl.BlockSpec((rows, bc), lambda b, c, s: (0, c))
    state = lambda rows: pl.BlockSpec((1, rows, bc), lambda b, c, s: (b, 0, c))
    gatew = pl.BlockSpec((nb, LRU_BW, LRU_BW), lambda b, c, s: (c, 0, 0))
    return pl.pallas_call(
        functools.partial(_lru_kernel, ts=ts, conv_w=conv_w),
        grid=(batch, nc, ns),
        in_specs=[tile(0), tile(1), tile(2), state(1), state(SUBLANES), chan(conv_w), chan(1),
                  gatew, chan(1), gatew, chan(1), chan(1)],
        out_specs=(pl.BlockSpec((ts, bc), lambda b, c, s: (b * ns + s, c)), state(1), state(SUBLANES)),
        out_shape=(jax.ShapeDtypeStruct((T, W), F32),
                   jax.ShapeDtypeStruct((batch, 1, W), F32),
                   jax.ShapeDtypeStruct((batch, SUBLANES, W), F32)),
        scratch_shapes=[pltpu.VMEM((ts + SUBLANES, bc), F32), pltpu.VMEM((1, bc), F32)],
        compiler_params=_cparams("parallel", "parallel", "arbitrary"),
        name="lru_branch",
    )(pb, pb, pb, h0, conv0p, cw, cb, wa, ba, wx, bx, lam)


def _merge_out_kernel(part_ref, gb_ref, attn_ref, x_ref, wo_ref, g_ref, b_ref, y_ref, yb_ref, *, alpha):
    merged = part_ref[...] + jax.nn.sigmoid(gb_ref[...]) * attn_ref[...]
    out = jnp.dot(merged.astype(BF16), wo_ref[...], preferred_element_type=F32)
    y = _layer_norm_rows(alpha * x_ref[...] + out, g_ref[...], b_ref[...])
    y_ref[...] = y
    yb_ref[...] = y.astype(BF16)


def merge_out(part, pb, attn, x, wo, g, b, alpha):
    T, D = x.shape
    W = part.shape[1]
    tm = _tile(T, 256)
    row = lambda i: (i, 0)
    const = lambda i: (0, 0)
    return pl.pallas_call(
        functools.partial(_merge_out_kernel, alpha=alpha),
        grid=(T // tm,),
        in_specs=[pl.BlockSpec((tm, W), row), pl.BlockSpec((tm, W), lambda i: (i, 3)),
                  pl.BlockSpec((tm, W), row), pl.BlockSpec((tm, D), row),
                  pl.BlockSpec(wo.shape, const), pl.BlockSpec((1, D), const), pl.BlockSpec((1, D), const)],
        out_specs=(pl.BlockSpec((tm, D), row), pl.BlockSpec((tm, D), row)),
        out_shape=(jax.ShapeDtypeStruct((T, D), F32), jax.ShapeDtypeStruct((T, D), BF16)),
        compiler_params=_cparams("parallel"),
        name="merge_out",
    )(part, pb, attn, x, wo, g, b)


def _ffn_kernel(te_ref, nt_ref, x_ref, w1_ref, w3_ref, w2_ref, o_ref):
    del te_ref

    @pl.when(pl.program_id(1) == 0)
    def _():
        o_ref[...] = jnp.zeros(o_ref.shape, o_ref.dtype)

    @pl.when(pl.program_id(0) < nt_ref[0])
    def _():
        x = x_ref[...]
        a = jnp.dot(x, w1_ref[0], preferred_element_type=F32)
        b = jnp.dot(x, w3_ref[0], preferred_element_type=F32)
        h = (a * jax.nn.sigmoid(a) * b).astype(BF16)
        o_ref[...] += jnp.dot(h, w2_ref[0], preferred_element_type=F32)


def grouped_ffn(xs, tile_expert, n_tiles, w1, w3, w2, tm):
    P, D = xs.shape
    F = w1.shape[2]
    tf = _tile(F, 512, LANES)
    nf = F // tf
    def live(i, nt):
        return jnp.minimum(i, nt[0] - 1)
    def ff(i, f, nt):
        return jnp.where(i < nt[0], f, nf - 1)
    grid_spec = pltpu.PrefetchScalarGridSpec(
        num_scalar_prefetch=2,
        grid=(P // tm, nf),
        in_specs=[pl.BlockSpec((tm, D), lambda i, f, te, nt: (live(i, nt), 0)),
                  pl.BlockSpec((1, D, tf), lambda i, f, te, nt: (te[live(i, nt)], 0, ff(i, f, nt))),
                  pl.BlockSpec((1, D, tf), lambda i, f, te, nt: (te[live(i, nt)], 0, ff(i, f, nt))),
                  pl.BlockSpec((1, tf, D), lambda i, f, te, nt: (te[live(i, nt)], ff(i, f, nt), 0))],
        out_specs=pl.BlockSpec((tm, D), lambda i, f, te, nt: (i, 0)),
    )
    return pl.pallas_call(
        _ffn_kernel,
        grid_spec=grid_spec,
        out_shape=jax.ShapeDtypeStruct((P, D), F32),
        compiler_params=_cparams("parallel", "arbitrary"),
        name="grouped_ffn",
    )(tile_expert, n_tiles, xs, w1, w3, w2)


def _resid_ln_kernel(x_ref, f_ref, g_ref, b_ref, y_ref, yb_ref, *, alpha):
    y = _layer_norm_rows(alpha * x_ref[...] + f_ref[...], g_ref[...], b_ref[...])
    y_ref[...] = y
    yb_ref[...] = y.astype(BF16)


def resid_ln(x, f, g, b, alpha):
    T, D = x.shape
    tm = _tile(T, 512)
    row = lambda i: (i, 0)
    const = lambda i: (0, 0)
    return pl.pallas_call(
        functools.partial(_resid_ln_kernel, alpha=alpha),
        grid=(T // tm,),
        in_specs=[pl.BlockSpec((tm, D), row), pl.BlockSpec((tm, D), row),
                  pl.BlockSpec((1, D), const), pl.BlockSpec((1, D), const)],
        out_specs=(pl.BlockSpec((tm, D), row), pl.BlockSpec((tm, D), row)),
        out_shape=(jax.ShapeDtypeStruct((T, D), F32), jax.ShapeDtypeStruct((T, D), BF16)),
        compiler_params=_cparams("parallel"),
        name="resid_ln",
    )(x, f, g, b)


def _router_kernel(x_ref, w_ref, idx_ref, gate_ref, *, n_experts):
    logits = jnp.dot(x_ref[...], w_ref[...], preferred_element_type=F32, precision=lax.Precision.HIGHEST)
    lane = lax.broadcasted_iota(jnp.int32, logits.shape, 1)
    logits = jnp.where(lane < n_experts, logits, -jnp.inf)
    m1 = jnp.max(logits, axis=1, keepdims=True)
    i1 = jnp.min(jnp.where(logits == m1, lane, LANES), axis=1, keepdims=True)
    rest = jnp.where(lane == i1, -jnp.inf, logits)
    m2 = jnp.max(rest, axis=1, keepdims=True)
    i2 = jnp.min(jnp.where(rest == m2, lane, LANES), axis=1, keepdims=True)
    e2 = jnp.exp(m2 - m1)
    g1 = 1.0 / (1.0 + e2)
    g2 = e2 / (1.0 + e2)
    idx_ref[...] = jnp.where(lane == 0, i1, jnp.where(lane == 1, i2, 0))
    gate_ref[...] = jnp.where(lane == 0, g1, jnp.where(lane == 1, g2, 0.0))


def router(x, w_router):
    T, D = x.shape
    E = w_router.shape[1]
    wp = jnp.pad(w_router, ((0, 0), (0, LANES - E)))
    tm = _tile(T, 512)
    row = lambda i: (i, 0)
    return pl.pallas_call(
        functools.partial(_router_kernel, n_experts=E),
        grid=(T // tm,),
        in_specs=[pl.BlockSpec((tm, D), row), pl.BlockSpec((D, LANES), lambda i: (0, 0))],
        out_specs=(pl.BlockSpec((tm, LANES), row), pl.BlockSpec((tm, LANES), row)),
        out_shape=(jax.ShapeDtypeStruct((T, LANES), jnp.int32), jax.ShapeDtypeStruct((T, LANES), F32)),
        compiler_params=_cparams("parallel"),
        name="router",
    )(x, wp)


DMA_ISSUE_UNROLL = 8


def _gather_kernel(tok_ref, x_hbm, o_ref, buf, sem, *, tg):
    i = pl.program_id(0)

    def start_tile(tile, slot):
        def issue(r, c):
            pltpu.make_async_copy(x_hbm.at[pl.ds(tok_ref[tile * tg + r], 1)], buf.at[slot, pl.ds(r, 1)],
                                  sem.at[slot]).start()
            return c
        lax.fori_loop(0, tg, issue, 0, unroll=DMA_ISSUE_UNROLL)

    @pl.when(i == 0)
    def _():
        start_tile(0, 0)

    @pl.when(i + 1 < pl.num_programs(0))
    def _():
        start_tile(i + 1, (i + 1) % 2)

    slot = i % 2
    pltpu.make_async_copy(x_hbm.at[pl.ds(0, tg)], buf.at[slot], sem.at[slot]).wait()
    o_ref[...] = buf[slot].astype(BF16)


def gather_rows(x, row_token, tg):
    P = row_token.shape[0]
    D = x.shape[1]
    grid_spec = pltpu.PrefetchScalarGridSpec(
        num_scalar_prefetch=1,
        grid=(P // tg,),
        in_specs=[pl.BlockSpec(memory_space=pl.ANY)],
        out_specs=pl.BlockSpec((tg, D), lambda i, tok: (i, 0)),
        scratch_shapes=[pltpu.VMEM((2, tg, D), F32), pltpu.SemaphoreType.DMA((2,))],
    )
    return pl.pallas_call(
        functools.partial(_gather_kernel, tg=tg),
        grid_spec=grid_spec,
        out_shape=jax.ShapeDtypeStruct((P, D), BF16),
        compiler_params=_cparams("arbitrary"),
        name="gather_rows",
    )(row_token, x)


def _combine_kernel(pos_ref, ys_hbm, gate_ref, x_ref, g_ref, b_ref, y_ref, yb_ref, buf, sem, *, tc, alpha):
    i = pl.program_id(0)

    def start_tile(tile, slot):
        def issue(r, c):
            p = (tile * tc + r) * TOP_K
            for k in range(TOP_K):
                pltpu.make_async_copy(ys_hbm.at[pl.ds(pos_ref[p + k], 1)], buf.at[slot, k, pl.ds(r, 1)],
                                      sem.at[slot, k]).start()
            return c
        lax.fori_loop(0, tc, issue, 0, unroll=DMA_ISSUE_UNROLL)

    @pl.when(i == 0)
    def _():
        start_tile(0, 0)

    @pl.when(i + 1 < pl.num_programs(0))
    def _():
        start_tile(i + 1, (i + 1) % 2)

    slot = i % 2
    for k in range(TOP_K):
        pltpu.make_async_copy(ys_hbm.at[pl.ds(0, tc)], buf.at[slot, k], sem.at[slot, k]).wait()
    gates = gate_ref[...]
    f = gates[:, 0:1] * buf[slot, 0] + gates[:, 1:2] * buf[slot, 1]
    y = _layer_norm_rows(alpha * x_ref[...] + f, g_ref[...], b_ref[...])
    y_ref[...] = y
    yb_ref[...] = y.astype(BF16)


def combine_ln(ys, pos, gates, x, g, b, alpha):
    T, D = x.shape
    tc = _tile(T, 256)
    row = lambda i, pos: (i, 0)
    const = lambda i, pos: (0, 0)
    grid_spec = pltpu.PrefetchScalarGridSpec(
        num_scalar_prefetch=1,
        grid=(T // tc,),
        in_specs=[pl.BlockSpec(memory_space=pl.ANY), pl.BlockSpec((tc, LANES), row),
                  pl.BlockSpec((tc, D), row), pl.BlockSpec((1, D), const), pl.BlockSpec((1, D), const)],
        out_specs=(pl.BlockSpec((tc, D), row), pl.BlockSpec((tc, D), row)),
        scratch_shapes=[pltpu.VMEM((2, TOP_K, tc, D), F32), pltpu.SemaphoreType.DMA((2, TOP_K))],
    )
    return pl.pallas_call(
        functools.partial(_combine_kernel, tc=tc, alpha=alpha),
        grid_spec=grid_spec,
        out_shape=(jax.ShapeDtypeStruct((T, D), F32), jax.ShapeDtypeStruct((T, D), BF16)),
        compiler_params=_cparams("arbitrary"),
        name="combine_ln",
    )(pos, ys, gates, x, g, b)


def moe_block(x, xb_unused, w_router, w1, w3, w2, g, b, alpha):
    del xb_unused
    T, D = x.shape
    E = w1.shape[0]
    tm = 1024 if T * TOP_K >= 8 * 1024 else LANES
    idx, gates = router(x, w_router)
    e_flat = idx[:, :TOP_K].reshape(-1)
    onehot = (e_flat[:, None] == jnp.arange(E, dtype=jnp.int32)[None, :]).astype(jnp.int32)
    rank = jnp.take_along_axis(jnp.cumsum(onehot, axis=0) - onehot, e_flat[:, None], axis=1)[:, 0]
    counts = jnp.sum(onehot, axis=0)
    padded = ((counts + tm - 1) // tm) * tm
    ends = jnp.cumsum(padded)
    pos = (ends - padded)[e_flat] + rank
    P = T * TOP_K + E * tm
    row_token = jnp.zeros((P,), jnp.int32).at[pos].set(jnp.arange(T * TOP_K, dtype=jnp.int32) // TOP_K)
    tile_start = jnp.arange(P // tm, dtype=jnp.int32) * tm
    tile_expert = jnp.minimum(jnp.sum(ends[None, :] <= tile_start[:, None], axis=1), E - 1).astype(jnp.int32)
    n_tiles = (ends[-1:] // tm).astype(jnp.int32)

    xs = gather_rows(x, row_token, _tile(tm, 512))
    ys = grouped_ffn(xs, tile_expert, n_tiles, w1, w3, w2, tm)
    return combine_ln(ys, pos.astype(jnp.int32), gates, x, g, b, alpha)


def dense_block(x, xb, w1, w3, w2, g, b, alpha):
    T = x.shape[0]
    tm = _tile(T, 1024)
    f = grouped_ffn(xb, jnp.zeros((T // tm,), jnp.int32), jnp.full((1,), T // tm, jnp.int32), w1, w3, w2, tm)
    return resid_ln(x, f, g, b, alpha)


def _rot_cols(w):
    return jnp.concatenate([-w[..., ROPE_HALF:], w[..., :ROPE_HALF]], axis=-1)


def _prep_layer(l, w_in, q_norm_g, w_uq, kv_norm_g, w_uk, w_uv, conv_w, conv_b, w_gate_a, b_gate_a,
                w_gate_x, b_gate_x, lru_lambda, w_o):
    D = w_in.shape[1]
    q_lora, kv_lora = q_norm_g.shape[1], kv_norm_g.shape[1]
    n_heads = w_uk.shape[2] // NOPE_DIM
    W = conv_w.shape[2]
    wi = w_in[l]
    o = q_lora + kv_lora
    kr = wi[:, o:o + ROPE_DIM]
    z = jnp.zeros((D, LANES - ROPE_DIM), F32)
    w_lat = jnp.concatenate([wi[:, :o], kr, z, _rot_cols(kr), z], axis=1).astype(BF16)
    w_b = wi[:, o + ROPE_DIM:].astype(BF16)
    uq = w_uq[l].reshape(q_lora, n_heads, QK_DIM)
    nope, ropew = uq[..., :NOPE_DIM], uq[..., NOPE_DIM:]
    zq = jnp.zeros((q_lora, n_heads, LANES - ROPE_DIM), F32)
    w1 = jnp.concatenate([nope, ropew, zq], axis=-1).reshape(q_lora, n_heads * HEAD_PAD).astype(BF16)
    w2 = jnp.concatenate([_rot_cols(ropew), zq], axis=-1).reshape(q_lora, n_heads * LANES).astype(BF16)
    wkv = jnp.concatenate([w_uk[l], w_uv[l]], axis=1).astype(BF16)
    return dict(
        w_lat=w_lat, w_b=w_b, qg=q_norm_g[l][None, :], kvg=kv_norm_g[l][None, :], w1=w1, w2=w2, wkv=wkv,
        n_heads=n_heads, cw=conv_w[l], cb=conv_b[l][None, :],
        wa=w_gate_a[l].astype(BF16), ba=b_gate_a[l].reshape(1, W),
        wx=w_gate_x[l].astype(BF16), bx=b_gate_x[l].reshape(1, W),
        lam=lru_lambda[l][None, :], wo=w_o[l].astype(BF16))


def _pad_ff(w1, w3, w2):
    F = w1.shape[-1]
    Fp = -(-F // 512) * 512
    if Fp != F:
        w1 = jnp.pad(w1, ((0, 0), (0, 0), (0, Fp - F)))
        w3 = jnp.pad(w3, ((0, 0), (0, 0), (0, Fp - F)))
        w2 = jnp.pad(w2, ((0, 0), (0, Fp - F), (0, 0)))
    return w1.astype(BF16), w3.astype(BF16), w2.astype(BF16)


def kernel(x_prompt, x_sample, cache_kv_latent, cache_k_rope, state_lru, state_conv, w_in, q_norm_g, w_uq, kv_norm_g, w_uk, w_uv, conv_w, conv_b, w_gate_a, b_gate_a, w_gate_x, b_gate_x, lru_lambda, w_o, ln1_g, ln1_b, ln2_g, ln2_b, ffn_w1, ffn_w3, ffn_w2, router_w, moe_w1, moe_w3, moe_w2):
    depth = w_in.shape[0]
    alpha = (2 * depth) ** 0.25
    B, S, D = x_prompt.shape
    Bs, Ss, _ = x_sample.shape
    past = cache_kv_latent.shape[2]
    W = conv_w.shape[2]
    conv_pad = SUBLANES - (conv_w.shape[1] - 1)

    layers = [_prep_layer(l, w_in, q_norm_g, w_uq, kv_norm_g, w_uk, w_uv, conv_w, conv_b, w_gate_a, b_gate_a,
                          w_gate_x, b_gate_x, lru_lambda, w_o) for l in range(depth)]
    ffn = []
    for l in range(depth):
        if l % 2 == 0:
            ffn.append(_pad_ff(ffn_w1[l // 2][None], ffn_w3[l // 2][None], ffn_w2[l // 2][None]))
        else:
            ffn.append(_pad_ff(moe_w1[l // 2], moe_w3[l // 2], moe_w2[l // 2]))

    cos_p, sin_p = rope_tables(S, 0)
    cos_s, sin_s = rope_tables(Ss, past)
    cos_s, sin_s = jnp.tile(cos_s, (Bs, 1)), jnp.tile(sin_s, (Bs, 1))

    def trunk(x3, cos, sin, pos0, past_lat, past_kr, h0, conv0):
        Bg, Sg, _ = x3.shape
        x = x3.reshape(Bg * Sg, D)
        xb = x
        lats, krs, hs, bufs = [], [], [], []
        for l in range(depth):
            lw = layers[l]
            H = lw["n_heads"]
            p_lat = matmul(xb, lw["w_lat"], F32, tn_pref=lw["w_lat"].shape[1])
            pb = matmul(xb, lw["w_b"], F32)
            q, kc, ve, lat, kr = latent_stage(p_lat, cos, sin, lw["qg"], lw["kvg"], lw["w1"], lw["w2"],
                                              lw["wkv"], H)
            if past_lat is None:
                h_in = jnp.zeros((Bg, 1, W), F32)
                c_in = jnp.zeros((Bg, SUBLANES, W), F32)
            else:
                n_past = past_lat.shape[2]
                kv_past = matmul(past_lat[l].reshape(Bg * n_past, -1), lw["wkv"], BF16)
                kv_past = kv_past.reshape(Bg, n_past, 2 * H, NOPE_DIM)
                kr_past = jnp.pad(past_kr[l], ((0, 0), (0, 0), (0, LANES - ROPE_DIM))).astype(BF16)
                kr_past = jnp.broadcast_to(kr_past[:, :, None, :], (Bg, n_past, H, LANES))
                kc_past = jnp.concatenate([kv_past[:, :, :H], kr_past], axis=-1).reshape(Bg, n_past, -1)
                ve_past = jnp.concatenate([kv_past[:, :, H:], jnp.ones_like(kr_past)], axis=-1)
                ve_past = ve_past.reshape(Bg, n_past, -1)
                kc = jnp.concatenate([kc_past, kc.reshape(Bg, Sg, -1)], axis=1).reshape(Bg * (n_past + Sg), -1)
                ve = jnp.concatenate([ve_past, ve.reshape(Bg, Sg, -1)], axis=1).reshape(Bg * (n_past + Sg), -1)
                h_in = h0[l][:, None, :]
                c_in = jnp.pad(conv0[l], ((0, 0), (conv_pad, 0), (0, 0)))
            attn = attention(q, kc, ve, Bg, H, pos0)
            part, h_last, conv_last = lru_branch(pb, h_in, c_in, lw["cw"], lw["cb"], lw["wa"], lw["ba"],
                                                 lw["wx"], lw["bx"], lw["lam"], Bg)
            x, xb = merge_out(part, pb, attn, x, lw["wo"], ln1_g[l][None, :], ln1_b[l][None, :], alpha)
            w1, w3, w2 = ffn[l]
            if l % 2 == 0:
                x, xb = dense_block(x, xb, w1, w3, w2, ln2_g[l][None, :], ln2_b[l][None, :], alpha)
            else:
                x, xb = moe_block(x, xb, router_w[l // 2], w1, w3, w2, ln2_g[l][None, :], ln2_b[l][None, :],
                                  alpha)
            lats.append(lat.reshape(Bg, Sg, -1))
            krs.append(kr.reshape(Bg, Sg, ROPE_DIM))
            hs.append(h_last[:, 0, :])
            bufs.append(conv_last[:, conv_pad:, :])
        return x.reshape(Bg, Sg, D), jnp.stack(lats), jnp.stack(krs), jnp.stack(hs), jnp.stack(bufs)

    y_p, p_lat, p_kr, p_lru, p_conv = trunk(x_prompt, cos_p, sin_p, 0, None, None, None, None)
    y_s, s_lat, s_kr, s_lru, s_conv = trunk(x_sample, cos_s, sin_s, past, cache_kv_latent, cache_k_rope,
                                            state_lru, state_conv)
    return (y_p, y_s, p_lat, p_kr, p_lru, p_conv, s_lat, s_kr, s_lru, s_conv)
```

```python
import functools
import math

import jax
import jax.numpy as jnp
from jax import lax
from jax.experimental import pallas as pl
from jax.experimental.pallas import tpu as pltpu

CHUNK = 64
NOPE_DIM = 128
ROPE_DIM = 64
ROPE_HALF = ROPE_DIM // 2
V_DIM = 128
QK_DIM = NOPE_DIM + ROPE_DIM
SCORE_SCALE = QK_DIM ** -0.5 * math.log2(math.e)
ROPE_THETA = 10000.0
LRU_BW = 128
LRU_C = 8.0
TOP_K = 2
LN_EPS = 1e-5
RMS_EPS = 1e-6

LANES = 128
SUBLANES = 8
HEAD_PAD = 2 * LANES
V7X_VMEM_BYTES = 64 * 1024 * 1024
VMEM_LIMIT = V7X_VMEM_BYTES - 8 * 1024 * 1024

BF16 = jnp.bfloat16
F32 = jnp.float32


def _cparams(*sem):
    return pltpu.CompilerParams(dimension_semantics=sem, vmem_limit_bytes=VMEM_LIMIT)


def _tile(n, pref, mult=SUBLANES):
    if n <= pref:
        return n
    t = (pref // mult) * mult
    while t >= mult:
        if n % t == 0:
            return t
        t -= mult
    return n


def _layer_norm_rows(y, g, b):
    mu = jnp.mean(y, axis=-1, keepdims=True)
    yc = y - mu
    var = jnp.mean(yc * yc, axis=-1, keepdims=True)
    return yc * lax.rsqrt(var + LN_EPS) * g + b


def _rms_norm_rows(y, g):
    return y * lax.rsqrt(jnp.mean(y * y, axis=-1, keepdims=True) + RMS_EPS) * g


def _mm_cast_kernel(x_ref, w_ref, o_ref, xb_ref):
    @pl.when(pl.program_id(1) == 0)
    def _():
        xb_ref[...] = x_ref[...].astype(BF16)

    o_ref[...] = jnp.dot(xb_ref[...], w_ref[...], preferred_element_type=F32).astype(o_ref.dtype)


def _mm_kernel(x_ref, w_ref, o_ref):
    o_ref[...] = jnp.dot(x_ref[...], w_ref[...], preferred_element_type=F32).astype(o_ref.dtype)


def matmul(x, w, out_dtype, tm_pref=1024, tn_pref=1024):
    M, K = x.shape
    N = w.shape[1]
    tm = _tile(M, tm_pref)
    tn = _tile(N, tn_pref, LANES)
    cast = x.dtype != BF16
    return pl.pallas_call(
        _mm_cast_kernel if cast else _mm_kernel,
        grid=(M // tm, N // tn),
        in_specs=[pl.BlockSpec((tm, K), lambda i, j: (i, 0)),
                  pl.BlockSpec((K, tn), lambda i, j: (0, j))],
        out_specs=pl.BlockSpec((tm, tn), lambda i, j: (i, j)),
        out_shape=jax.ShapeDtypeStruct((M, N), out_dtype),
        scratch_shapes=[pltpu.VMEM((tm, K), BF16)] if cast else [],
        compiler_params=_cparams("parallel", "arbitrary"),
        name="matmul",
    )(x, w)


def _rope_table_kernel(inv_ref, cos_ref, sin_ref, *, pos0):
    rows = lax.broadcasted_iota(jnp.int32, cos_ref.shape, 0)
    lane = lax.broadcasted_iota(jnp.int32, cos_ref.shape, 1)
    ang = (rows + pos0).astype(F32) * inv_ref[...]
    live = lane < ROPE_DIM
    cos_ref[...] = jnp.where(live, jnp.cos(ang), 0.0)
    sin_ref[...] = jnp.where(live, jnp.sin(ang), 0.0)


def rope_tables(seq, pos0):
    inv = ROPE_THETA ** (-jnp.arange(ROPE_HALF, dtype=F32) / ROPE_HALF)
    inv128 = jnp.concatenate([inv, inv, jnp.zeros((LANES - ROPE_DIM,), F32)])[None, :]
    return pl.pallas_call(
        functools.partial(_rope_table_kernel, pos0=pos0),
        out_shape=(jax.ShapeDtypeStruct((seq, LANES), F32),) * 2,
        name="rope_tables",
    )(inv128)


def _write_kv(lat, krope, wkv_ref, kc_ref, ve_ref, n_heads):
    krope_b = krope.astype(BF16)
    ones = jnp.ones(krope_b.shape, BF16)
    kv = jnp.dot(lat.astype(BF16), wkv_ref[...], preferred_element_type=F32).astype(BF16)
    for h in range(n_heads):
        lo, hi = h * HEAD_PAD, h * HEAD_PAD + LANES
        kc_ref[:, lo:hi] = kv[:, h * NOPE_DIM:(h + 1) * NOPE_DIM]
        kc_ref[:, hi:hi + LANES] = krope_b
        ve_ref[:, lo:hi] = kv[:, (n_heads + h) * V_DIM:(n_heads + h + 1) * V_DIM]
        ve_ref[:, hi:hi + LANES] = ones


def _cache_kv_kernel(lat_ref, krp_ref, wkv_ref, kc_ref, ve_ref, *, n_heads):
    _write_kv(lat_ref[...], krp_ref[...], wkv_ref, kc_ref, ve_ref, n_heads)


def cache_kv(lat, krp, wkv, n_heads):
    R = lat.shape[0]
    tm = _tile(R, 512)
    row = lambda i: (i, 0)
    head_cols = n_heads * HEAD_PAD
    return pl.pallas_call(
        functools.partial(_cache_kv_kernel, n_heads=n_heads),
        grid=(R // tm,),
        in_specs=[pl.BlockSpec((tm, lat.shape[1]), row), pl.BlockSpec((tm, LANES), row),
                  pl.BlockSpec(wkv.shape, lambda i: (0, 0))],
        out_specs=(pl.BlockSpec((tm, head_cols), row), pl.BlockSpec((tm, head_cols), row)),
        out_shape=(jax.ShapeDtypeStruct((R, head_cols), BF16), jax.ShapeDtypeStruct((R, head_cols), BF16)),
        compiler_params=_cparams("parallel"),
        name="cache_kv",
    )(lat, krp, wkv)


def _latent_kernel(p_ref, cos_ref, sin_ref, qg_ref, kvg_ref, w1_ref, w2_ref, wkv_ref,
                   q_ref, kc_ref, ve_ref, lat_ref, kr_ref, *, q_lora, kv_lora, n_heads):
    cos = cos_ref[...]
    sin = sin_ref[...]
    cqn = _rms_norm_rows(p_ref[:, :q_lora], qg_ref[...]).astype(BF16)
    lat = _rms_norm_rows(p_ref[:, q_lora:q_lora + kv_lora], kvg_ref[...])
    lat_ref[...] = lat
    o = q_lora + kv_lora
    krope = p_ref[:, o:o + LANES] * cos + p_ref[:, o + LANES:o + 2 * LANES] * sin
    kr_ref[...] = krope[:, :ROPE_DIM]
    _write_kv(lat, krope, wkv_ref, kc_ref, ve_ref, n_heads)
    for h in range(n_heads):
        lo, hi = h * HEAD_PAD, h * HEAD_PAD + LANES
        qa = jnp.dot(cqn, w1_ref[:, lo:lo + HEAD_PAD], preferred_element_type=F32)
        qb = jnp.dot(cqn, w2_ref[:, h * LANES:(h + 1) * LANES], preferred_element_type=F32)
        q_ref[:, lo:hi] = (qa[:, :LANES] * SCORE_SCALE).astype(BF16)
        q_ref[:, hi:hi + LANES] = ((qa[:, LANES:] * cos + qb * sin) * SCORE_SCALE).astype(BF16)


def latent_stage(p, cos, sin, qg, kvg, w1, w2, wkv, n_heads):
    T = p.shape[0]
    q_lora, kv_lora = qg.shape[1], kvg.shape[1]
    tm = _tile(min(T, cos.shape[0]), 256)
    n_pos = cos.shape[0] // tm
    const = lambda i: (0, 0)
    row = lambda i: (i, 0)
    head_cols = n_heads * HEAD_PAD
    return pl.pallas_call(
        functools.partial(_latent_kernel, q_lora=q_lora, kv_lora=kv_lora, n_heads=n_heads),
        grid=(T // tm,),
        in_specs=[pl.BlockSpec((tm, p.shape[1]), row),
                  pl.BlockSpec((tm, LANES), lambda i: (i % n_pos, 0)),
                  pl.BlockSpec((tm, LANES), lambda i: (i % n_pos, 0)),
                  pl.BlockSpec(qg.shape, const), pl.BlockSpec(kvg.shape, const),
                  pl.BlockSpec(w1.shape, const), pl.BlockSpec(w2.shape, const),
                  pl.BlockSpec(wkv.shape, const)],
        out_specs=(pl.BlockSpec((tm, head_cols), row),
                   pl.BlockSpec((tm, head_cols), row),
                   pl.BlockSpec((tm, head_cols), row),
                   pl.BlockSpec((tm, kv_lora), row),
                   pl.BlockSpec((tm, ROPE_DIM), row)),
        out_shape=(jax.ShapeDtypeStruct((T, head_cols), BF16),
                   jax.ShapeDtypeStruct((T, head_cols), BF16),
                   jax.ShapeDtypeStruct((T, head_cols), BF16),
                   jax.ShapeDtypeStruct((T, kv_lora), F32),
                   jax.ShapeDtypeStruct((T, ROPE_DIM), F32)),
        compiler_params=_cparams("parallel"),
        name="latent_stage",
    )(p, cos, sin, qg, kvg, w1, w2, wkv)


ATTN_TK = 512
ATTN_SM_ROWS = 128
ATTN_PV_ROWS = 256


def _chunk_mask(s, q0, k0):
    qpos = q0 + lax.broadcasted_iota(jnp.int32, s.shape, 0)
    kpos = k0 + lax.broadcasted_iota(jnp.int32, s.shape, 1)
    return jnp.where(kpos // CHUNK <= qpos // CHUNK, s, -jnp.inf)


def _attn_out(acc_sc):
    return acc_sc[:, :V_DIM] / acc_sc[:, V_DIM:]


def _attn_aligned_kernel(q_ref, k_ref, v_ref, o_ref, m_sc, acc_sc, s0, s1, p_sc, *, tq, tk, q_pos0):
    row0 = q_pos0 + pl.program_id(2) * tq
    n_pairs = row0 // (2 * tk)
    n_groups = tq // ATTN_PV_ROWS
    lower = tuple(range(n_groups // 2))
    upper = tuple(range(n_groups // 2, n_groups))

    m_sc[...] = jnp.full(m_sc.shape, -jnp.inf, F32)
    acc_sc[...] = jnp.zeros(acc_sc.shape, F32)

    def qk(start, s_ref, rows=slice(None)):
        s_ref[rows, :] = lax.dot_general(q_ref[rows, :], k_ref[pl.ds(start, tk), :], (((1,), (1,)), ((), ())),
                                         preferred_element_type=F32)

    def sm_pv(start, s_ref, groups, masked):
        for g in groups:
            alphas = []
            for r in range(ATTN_PV_ROWS // ATTN_SM_ROWS):
                r0 = g * ATTN_PV_ROWS + r * ATTN_SM_ROWS
                rs = slice(r0, r0 + ATTN_SM_ROWS)
                s = s_ref[rs, :]
                if masked:
                    s = _chunk_mask(s, r0 % tk, 0)
                m_prev = m_sc[rs, :]
                m_new = jnp.maximum(m_prev, jnp.max(s, axis=1, keepdims=True))
                p_sc[rs, :] = jnp.exp2(s - m_new[:, :1]).astype(BF16)
                alphas.append(jnp.exp2(m_prev - m_new))
                m_sc[rs, :] = m_new
            gs = slice(g * ATTN_PV_ROWS, (g + 1) * ATTN_PV_ROWS)
            alpha = jnp.concatenate(alphas, axis=0)
            acc_sc[gs, :] = (jnp.concatenate([alpha, alpha], axis=1) * acc_sc[gs, :]
                             + jnp.dot(p_sc[gs, :], v_ref[pl.ds(start, tk), :], preferred_element_type=F32))

    qk(0, s0)

    def pair(t):
        a = pl.multiple_of(2 * t * tk, 2 * tk)
        qk(a + tk, s1)
        sm_pv(a, s0, lower + upper, False)
        qk(a + 2 * tk, s0)
        sm_pv(a + tk, s1, lower + upper, False)

    def two_pairs(t, c):
        pair(2 * t)
        pair(2 * t + 1)
        return c

    lax.fori_loop(0, n_pairs // 2, two_pairs, 0)

    @pl.when(n_pairs % 2 == 1)
    def _():
        pair(n_pairs - 1)

    d0 = pl.multiple_of(2 * n_pairs * tk, 2 * tk)
    qk(d0 + tk, s1, slice(tq // 2, tq))
    sm_pv(d0, s0, lower, True)
    sm_pv(d0, s0, upper, False)
    sm_pv(d0 + tk, s1, upper, True)
    o_ref[...] = _attn_out(acc_sc)


def _attn_general_kernel(q_ref, k_ref, v_ref, o_ref, m_sc, acc_sc, *, tq, tk, sk, q_pos0):
    row0 = q_pos0 + pl.program_id(2) * tq
    first_vis = (row0 // CHUNK + 1) * CHUNK
    last_vis = ((row0 + tq - 1) // CHUNK + 1) * CHUNK
    kv_hi = jnp.minimum(last_vis, sk)
    n_full = sk // tk
    tail = sk - n_full * tk
    n_unmasked = jnp.minimum(first_vis, sk) // tk
    n_masked_end = jnp.minimum((kv_hi + tk - 1) // tk, n_full)

    m_sc[...] = jnp.full(m_sc.shape, -jnp.inf, F32)
    acc_sc[...] = jnp.zeros(acc_sc.shape, F32)

    def step(start, size, masked):
        s = lax.dot_general(q_ref[...], k_ref[pl.ds(start, size), :], (((1,), (1,)), ((), ())),
                            preferred_element_type=F32)
        if masked:
            s = _chunk_mask(s, row0, start)
        m_prev = m_sc[...]
        m_new = jnp.maximum(m_prev, jnp.max(s, axis=1, keepdims=True))
        p = jnp.exp2(s - m_new[:, :1]).astype(BF16)
        alpha = jnp.exp2(m_prev - m_new)
        acc_sc[...] = (jnp.concatenate([alpha, alpha], axis=1) * acc_sc[...]
                       + jnp.dot(p, v_ref[pl.ds(start, size), :], preferred_element_type=F32))
        m_sc[...] = m_new

    def loop(lo, hi, masked):
        def body(j, c):
            step(pl.multiple_of(j * tk, tk), tk, masked)
            return c
        lax.fori_loop(lo, hi, body, 0)

    loop(0, n_unmasked, False)
    loop(n_unmasked, n_masked_end, True)
    if tail:
        @pl.when(kv_hi > n_full * tk)
        def _():
            step(n_full * tk, tail, True)

    o_ref[...] = _attn_out(acc_sc)


def attention(q, kc, ve, batch, n_heads, q_pos0):
    sq = q.shape[0] // batch
    sk = kc.shape[0] // batch
    tk = min(ATTN_TK, sk)
    aligned = sq % (2 * tk) == 0 and q_pos0 % (2 * tk) == 0 and sk == q_pos0 + sq
    tq = 2 * tk if aligned else _tile(sq, ATTN_TK)
    nq = sq // tq
    scratch = [pltpu.VMEM((tq, LANES), F32), pltpu.VMEM((tq, 2 * V_DIM), F32)]
    if aligned:
        body = functools.partial(_attn_aligned_kernel, tq=tq, tk=tk, q_pos0=q_pos0)
        scratch += [pltpu.VMEM((tq, tk), F32), pltpu.VMEM((tq, tk), F32), pltpu.VMEM((tq, tk), BF16)]
    else:
        body = functools.partial(_attn_general_kernel, tq=tq, tk=tk, sk=sk, q_pos0=q_pos0)
    return pl.pallas_call(
        body,
        grid=(batch, n_heads, nq),
        in_specs=[pl.BlockSpec((tq, HEAD_PAD), lambda b, h, i: (b * nq + i, h)),
                  pl.BlockSpec((sk, HEAD_PAD), lambda b, h, i: (b, h)),
                  pl.BlockSpec((sk, 2 * V_DIM), lambda b, h, i: (b, h))],
        out_specs=pl.BlockSpec((tq, V_DIM), lambda b, h, i: (b * nq + i, h)),
        out_shape=jax.ShapeDtypeStruct((batch * sq, n_heads * V_DIM), F32),
        scratch_shapes=scratch,
        compiler_params=_cparams("parallel", "parallel", "arbitrary"),
        name="attention",
    )(q, kc, ve)


def _lru_kernel(u_ref, gate_ref, ga_ref, h0_ref, conv0_ref, cw_ref, cb_ref, wa_ref, ba_ref,
                wx_ref, bx_ref, lam_ref, part_ref, hlast_ref, convlast_ref, ubuf, hc, *, ts, conv_w):
    pad = SUBLANES

    @pl.when(pl.program_id(2) == 0)
    def _():
        ubuf[...] = conv0_ref[0]
        hc[...] = h0_ref[0]

    u = u_ref[...]
    prev = ubuf[...]
    head_row = lax.broadcasted_iota(jnp.int32, prev.shape, 0)
    uc = cb_ref[...]
    for j in range(conv_w):
        k = conv_w - 1 - j
        if k == 0:
            shifted = u
        else:
            rolled = pltpu.roll(u, k, 0)
            head = jnp.where(head_row < k, pltpu.roll(prev, k, 0), rolled[:pad, :])
            shifted = jnp.concatenate([head, rolled[pad:, :]], axis=0)
        uc = uc + shifted * cw_ref[j:j + 1, :]
    ubuf[...] = u[ts - pad:, :]
    convlast_ref[0] = u[ts - pad:, :]

    ucb = uc.astype(BF16)
    bc = uc.shape[1]
    r_parts, i_parts = [], []
    for n in range(bc // LRU_BW):
        blk = ucb[:, n * LRU_BW:(n + 1) * LRU_BW]
        r_parts.append(jnp.dot(blk, wa_ref[n], preferred_element_type=F32))
        i_parts.append(jnp.dot(blk, wx_ref[n], preferred_element_type=F32))
    r = jax.nn.sigmoid(jnp.concatenate(r_parts, axis=1) + ba_ref[...])
    ig = jax.nn.sigmoid(jnp.concatenate(i_parts, axis=1) + bx_ref[...])

    neg_lam = -lam_ref[...]
    softplus = jnp.maximum(neg_lam, 0.0) + jnp.log1p(jnp.exp(-jnp.abs(neg_lam)))
    log_a = (-LRU_C * r) * softplus
    a = jnp.exp(log_a)
    z = -jnp.tanh(log_a) * (a * a + 1.0)
    b = jnp.where(z > 0.0, z * lax.rsqrt(z), 0.0) * (ig * uc)

    row_in_group = lax.broadcasted_iota(jnp.int32, a.shape, 0) % SUBLANES
    d = 1
    while d < SUBLANES:
        keep = row_in_group >= d
        a_prev = jnp.where(keep, pltpu.roll(a, d, 0), 1.0)
        b_prev = jnp.where(keep, pltpu.roll(b, d, 0), 0.0)
        b = a * b_prev + b
        a = a * a_prev
        d *= 2
    carry = hc[...]
    groups = []
    for g in range(ts // SUBLANES):
        rows = slice(g * SUBLANES, (g + 1) * SUBLANES)
        hg = b[rows, :] + a[rows, :] * carry
        carry = hg[SUBLANES - 1:SUBLANES, :]
        groups.append(hg)
    h = jnp.concatenate(groups, axis=0)
    hc[...] = carry
    hlast_ref[0] = carry
    part_ref[...] = jax.nn.sigmoid(ga_ref[...]) * (h * jax.nn.gelu(gate_ref[...]))


def lru_branch(pb, h0, conv0p, cw, cb, wa, ba, wx, bx, lam, batch):
    T = pb.shape[0]
    W = pb.shape[1] // 4
    S = T // batch
    ts = _tile(S, 256)
    bc = min(W, 512)
    ns, nc = S // ts, W // bc
    nb = bc // LRU_BW
    conv_w = cw.shape[0]
    tile = lambda g: pl.BlockSpec((ts, bc), lambda b, c, s: (b * ns + s, g * nc + c))
    chan = lambda rows: pl.BlockSpec((rows, bc), lambda b, c, s: (0, c))
    state = lambda rows: pl.BlockSpec((1, rows, bc), lambda b, c, s: (b, 0, c))
    gatew = pl.BlockSpec((nb, LRU_BW, LRU_BW), lambda b, c, s: (c, 0, 0))
    return pl.pallas_call(
        functools.partial(_lru_kernel, ts=ts, conv_w=conv_w),
        grid=(batch, nc, ns),
        in_specs=[tile(0), tile(1), tile(2), state(1), state(SUBLANES), chan(conv_w), chan(1),
                  gatew, chan(1), gatew, chan(1), chan(1)],
        out_specs=(pl.BlockSpec((ts, bc), lambda b, c, s: (b * ns + s, c)), state(1), state(SUBLANES)),
        out_shape=(jax.ShapeDtypeStruct((T, W), F32),
                   jax.ShapeDtypeStruct((batch, 1, W), F32),
                   jax.ShapeDtypeStruct((batch, SUBLANES, W), F32)),
        scratch_shapes=[pltpu.VMEM((SUBLANES, bc), F32), pltpu.VMEM((1, bc), F32)],
        compiler_params=_cparams("parallel", "parallel", "arbitrary"),
        name="lru_branch",
    )(pb, pb, pb, h0, conv0p, cw, cb, wa, ba, wx, bx, lam)


def _pack_bf16_pairs(y):
    half = y.shape[1] // 2
    lo = pltpu.bitcast(y[:, :half].astype(BF16).astype(F32), jnp.uint32) >> 16
    hi = pltpu.bitcast(y[:, half:].astype(BF16).astype(F32), jnp.uint32) & jnp.uint32(0xFFFF0000)
    return lo | hi


def _unpack_bf16_pairs(w, xb_ref):
    half = w.shape[1]
    xb_ref[:, :half] = pltpu.bitcast(w << 16, F32).astype(BF16)
    xb_ref[:, half:] = pltpu.bitcast(w & jnp.uint32(0xFFFF0000), F32).astype(BF16)


def _merge_out_kernel(part_ref, gb_ref, attn_ref, x_ref, wo_ref, g_ref, b_ref, y_ref, yb_ref, *, alpha, packed):
    merged = part_ref[...] + jax.nn.sigmoid(gb_ref[...]) * attn_ref[...]
    out = jnp.dot(merged.astype(BF16), wo_ref[...], preferred_element_type=F32)
    y = _layer_norm_rows(alpha * x_ref[...] + out, g_ref[...], b_ref[...])
    y_ref[...] = y
    yb_ref[...] = _pack_bf16_pairs(y) if packed else y.astype(BF16)


def merge_out(part, pb, attn, x, wo, g, b, alpha, packed):
    T, D = x.shape
    W = part.shape[1]
    tm = _tile(T, 256)
    row = lambda i: (i, 0)
    second = ((T, D // 2), jnp.uint32) if packed else ((T, D), BF16)
    const = lambda i: (0, 0)
    return pl.pallas_call(
        functools.partial(_merge_out_kernel, alpha=alpha, packed=packed),
        grid=(T // tm,),
        in_specs=[pl.BlockSpec((tm, W), row), pl.BlockSpec((tm, W), lambda i: (i, 3)),
                  pl.BlockSpec((tm, W), row), pl.BlockSpec((tm, D), row),
                  pl.BlockSpec(wo.shape, const), pl.BlockSpec((1, D), const), pl.BlockSpec((1, D), const)],
        out_specs=(pl.BlockSpec((tm, D), row), pl.BlockSpec((tm, second[0][1]), row)),
        out_shape=(jax.ShapeDtypeStruct((T, D), F32), jax.ShapeDtypeStruct(*second)),
        compiler_params=_cparams("parallel"),
        name="merge_out",
    )(part, pb, attn, x, wo, g, b)


def _ffn_kernel(te_ref, nt_ref, x_ref, w1_ref, w3_ref, w2_ref, o_ref, *unpacked):
    del te_ref
    live = pl.program_id(0) < nt_ref[0]

    @pl.when(pl.program_id(1) == 0)
    def _():
        o_ref[...] = jnp.zeros(o_ref.shape, o_ref.dtype)

    if unpacked:
        @pl.when(live & (pl.program_id(1) == 0))
        def _():
            _unpack_bf16_pairs(x_ref[...], unpacked[0])

    @pl.when(live)
    def _():
        x = unpacked[0][...] if unpacked else x_ref[...]
        a = jnp.dot(x, w1_ref[0], preferred_element_type=F32)
        b = jnp.dot(x, w3_ref[0], preferred_element_type=F32)
        h = (a * jax.nn.sigmoid(a) * b).astype(BF16)
        o_ref[...] += jnp.dot(h, w2_ref[0], preferred_element_type=F32)


def grouped_ffn(xs, tile_expert, n_tiles, w1, w3, w2, tm):
    P = xs.shape[0]
    D = w1.shape[1]
    packed = xs.dtype == jnp.uint32
    F = w1.shape[2]
    tf = _tile(F, 512, LANES)
    nf = F // tf
    def live(i, nt):
        return jnp.minimum(i, nt[0] - 1)
    def ff(i, f, nt):
        return jnp.where(i < nt[0], f, nf - 1)
    grid_spec = pltpu.PrefetchScalarGridSpec(
        num_scalar_prefetch=2,
        grid=(P // tm, nf),
        in_specs=[pl.BlockSpec((tm, xs.shape[1]), lambda i, f, te, nt: (live(i, nt), 0)),
                  pl.BlockSpec((1, D, tf), lambda i, f, te, nt: (te[live(i, nt)], 0, ff(i, f, nt))),
                  pl.BlockSpec((1, D, tf), lambda i, f, te, nt: (te[live(i, nt)], 0, ff(i, f, nt))),
                  pl.BlockSpec((1, tf, D), lambda i, f, te, nt: (te[live(i, nt)], ff(i, f, nt), 0))],
        out_specs=pl.BlockSpec((tm, D), lambda i, f, te, nt: (i, 0)),
        scratch_shapes=[pltpu.VMEM((tm, D), BF16)] if packed else [],
    )
    return pl.pallas_call(
        _ffn_kernel,
        grid_spec=grid_spec,
        out_shape=jax.ShapeDtypeStruct((P, D), F32),
        compiler_params=_cparams("parallel", "arbitrary"),
        name="grouped_ffn",
    )(tile_expert, n_tiles, xs, w1, w3, w2)


def _resid_ln_kernel(x_ref, f_ref, g_ref, b_ref, y_ref, yb_ref, *, alpha):
    y = _layer_norm_rows(alpha * x_ref[...] + f_ref[...], g_ref[...], b_ref[...])
    y_ref[...] = y
    yb_ref[...] = y.astype(BF16)


def resid_ln(x, f, g, b, alpha):
    T, D = x.shape
    tm = _tile(T, 512)
    row = lambda i: (i, 0)
    const = lambda i: (0, 0)
    return pl.pallas_call(
        functools.partial(_resid_ln_kernel, alpha=alpha),
        grid=(T // tm,),
        in_specs=[pl.BlockSpec((tm, D), row), pl.BlockSpec((tm, D), row),
                  pl.BlockSpec((1, D), const), pl.BlockSpec((1, D), const)],
        out_specs=(pl.BlockSpec((tm, D), row), pl.BlockSpec((tm, D), row)),
        out_shape=(jax.ShapeDtypeStruct((T, D), F32), jax.ShapeDtypeStruct((T, D), BF16)),
        compiler_params=_cparams("parallel"),
        name="resid_ln",
    )(x, f, g, b)


def _router_kernel(x_ref, w_ref, idx_ref, gate_ref, cnt_ref, run_sc, *, n_experts):
    @pl.when(pl.program_id(0) == 0)
    def _():
        run_sc[...] = jnp.zeros(run_sc.shape, F32)

    logits = jnp.dot(x_ref[...], w_ref[...], preferred_element_type=F32, precision=lax.Precision.HIGHEST)
    tm = logits.shape[0]
    lane = lax.broadcasted_iota(jnp.int32, logits.shape, 1)
    logits = jnp.where(lane < n_experts, logits, -jnp.inf)
    m1 = jnp.max(logits, axis=1, keepdims=True)
    i1 = jnp.min(jnp.where(logits == m1, lane, LANES), axis=1, keepdims=True)
    rest = jnp.where(lane == i1, -jnp.inf, logits)
    m2 = jnp.max(rest, axis=1, keepdims=True)
    i2 = jnp.min(jnp.where(rest == m2, lane, LANES), axis=1, keepdims=True)
    e2 = jnp.exp(m2 - m1)
    g1 = 1.0 / (1.0 + e2)
    g2 = e2 / (1.0 + e2)
    gate_ref[...] = jnp.where(lane == 0, g1, jnp.where(lane == 1, g2, 0.0))

    hot1 = jnp.where(lane == i1, 1.0, 0.0)
    hot2 = jnp.where(lane == i2, 1.0, 0.0)
    before = (lax.broadcasted_iota(jnp.int32, (tm, tm), 1) < lax.broadcasted_iota(jnp.int32, (tm, tm), 0))
    before = jnp.where(before, 1.0, 0.0).astype(BF16)
    c1 = jnp.dot(before, hot1.astype(BF16), preferred_element_type=F32)
    c2 = jnp.dot(before, hot2.astype(BF16), preferred_element_type=F32)
    n1 = jnp.sum(hot1, axis=0, keepdims=True)
    n2 = jnp.sum(hot2, axis=0, keepdims=True)
    run = run_sc[...]
    rank1 = jnp.sum(hot1 * (c1 + run), axis=1, keepdims=True).astype(jnp.int32)
    rank2 = jnp.sum(hot2 * (c2 + run + n1), axis=1, keepdims=True).astype(jnp.int32)
    run_sc[...] = run + n1 + n2
    cnt_ref[...] = run_sc[...].astype(jnp.int32)
    idx_ref[...] = jnp.where(lane == 0, i1, jnp.where(lane == 1, i2,
                             jnp.where(lane == 2, rank1, jnp.where(lane == 3, rank2, 0))))


def router(x, w_router):
    T, D = x.shape
    E = w_router.shape[1]
    wp = jnp.pad(w_router, ((0, 0), (0, LANES - E)))
    tm = _tile(T, 512)
    row = lambda i: (i, 0)
    return pl.pallas_call(
        functools.partial(_router_kernel, n_experts=E),
        grid=(T // tm,),
        in_specs=[pl.BlockSpec((tm, D), row), pl.BlockSpec((D, LANES), lambda i: (0, 0))],
        out_specs=(pl.BlockSpec((tm, LANES), row), pl.BlockSpec((tm, LANES), row),
                   pl.BlockSpec((1, LANES), lambda i: (0, 0))),
        out_shape=(jax.ShapeDtypeStruct((T, LANES), jnp.int32), jax.ShapeDtypeStruct((T, LANES), F32),
                   jax.ShapeDtypeStruct((1, LANES), jnp.int32)),
        scratch_shapes=[pltpu.VMEM((1, LANES), F32)],
        compiler_params=_cparams("arbitrary"),
        name="router",
    )(x, wp)


DMA_ISSUE_UNROLL = 8


def _gather_kernel(tok_ref, x_hbm, o_ref, buf, sem, *, tg):
    i = pl.program_id(0)

    def start_tile(tile, slot):
        def issue(r, c):
            pltpu.make_async_copy(x_hbm.at[pl.ds(tok_ref[tile * tg + r], 1)], buf.at[slot, pl.ds(r, 1)],
                                  sem.at[slot]).start()
            return c
        lax.fori_loop(0, tg, issue, 0, unroll=DMA_ISSUE_UNROLL)

    @pl.when(i == 0)
    def _():
        start_tile(0, 0)

    @pl.when(i + 1 < pl.num_programs(0))
    def _():
        start_tile(i + 1, (i + 1) % 2)

    slot = i % 2
    pltpu.make_async_copy(x_hbm.at[pl.ds(0, tg)], buf.at[slot], sem.at[slot]).wait()
    o_ref[...] = buf[slot]


def gather_rows(x, row_token, tg):
    P = row_token.shape[0]
    D = x.shape[1]
    grid_spec = pltpu.PrefetchScalarGridSpec(
        num_scalar_prefetch=1,
        grid=(P // tg,),
        in_specs=[pl.BlockSpec(memory_space=pl.ANY)],
        out_specs=pl.BlockSpec((tg, D), lambda i, tok: (i, 0)),
        scratch_shapes=[pltpu.VMEM((2, tg, D), x.dtype), pltpu.SemaphoreType.DMA((2,))],
    )
    return pl.pallas_call(
        functools.partial(_gather_kernel, tg=tg),
        grid_spec=grid_spec,
        out_shape=jax.ShapeDtypeStruct((P, D), x.dtype),
        compiler_params=_cparams("arbitrary"),
        name="gather_rows",
    )(row_token, x)


def _combine_kernel(pos_ref, ys_hbm, gate_ref, x_ref, g_ref, b_ref, y_ref, yb_ref, buf, sem, *, tc, alpha):
    i = pl.program_id(0)

    def start_tile(tile, slot):
        def issue(r, c):
            p = (tile * tc + r) * TOP_K
            for k in range(TOP_K):
                pltpu.make_async_copy(ys_hbm.at[pl.ds(pos_ref[p + k], 1)], buf.at[slot, k, pl.ds(r, 1)],
                                      sem.at[slot, k]).start()
            return c
        lax.fori_loop(0, tc, issue, 0, unroll=DMA_ISSUE_UNROLL)

    @pl.when(i == 0)
    def _():
        start_tile(0, 0)

    @pl.when(i + 1 < pl.num_programs(0))
    def _():
        start_tile(i + 1, (i + 1) % 2)

    slot = i % 2
    for k in range(TOP_K):
        pltpu.make_async_copy(ys_hbm.at[pl.ds(0, tc)], buf.at[slot, k], sem.at[slot, k]).wait()
    gates = gate_ref[...]
    f = gates[:, 0:1] * buf[slot, 0] + gates[:, 1:2] * buf[slot, 1]
    y = _layer_norm_rows(alpha * x_ref[...] + f, g_ref[...], b_ref[...])
    y_ref[...] = y
    yb_ref[...] = y.astype(BF16)


def combine_ln(ys, pos, gates, x, g, b, alpha):
    T, D = x.shape
    tc = _tile(T, 256)
    row = lambda i, pos: (i, 0)
    const = lambda i, pos: (0, 0)
    grid_spec = pltpu.PrefetchScalarGridSpec(
        num_scalar_prefetch=1,
        grid=(T // tc,),
        in_specs=[pl.BlockSpec(memory_space=pl.ANY), pl.BlockSpec((tc, LANES), row),
                  pl.BlockSpec((tc, D), row), pl.BlockSpec((1, D), const), pl.BlockSpec((1, D), const)],
        out_specs=(pl.BlockSpec((tc, D), row), pl.BlockSpec((tc, D), row)),
        scratch_shapes=[pltpu.VMEM((2, TOP_K, tc, D), F32), pltpu.SemaphoreType.DMA((2, TOP_K))],
    )
    return pl.pallas_call(
        functools.partial(_combine_kernel, tc=tc, alpha=alpha),
        grid_spec=grid_spec,
        out_shape=(jax.ShapeDtypeStruct((T, D), F32), jax.ShapeDtypeStruct((T, D), BF16)),
        compiler_params=_cparams("arbitrary"),
        name="combine_ln",
    )(pos, ys, gates, x, g, b)


def moe_block(x, xp, w_router, w1, w3, w2, g, b, alpha):
    T, D = x.shape
    E = w1.shape[0]
    tm = 1024 if T * TOP_K >= 8 * 1024 else LANES
    idx, gates, counts = router(x, w_router)
    e_flat = idx[:, :TOP_K].reshape(-1)
    rank = idx[:, TOP_K:2 * TOP_K].reshape(-1)
    counts = counts[0, :E]
    padded = ((counts + tm - 1) // tm) * tm
    ends = jnp.cumsum(padded)
    pos = (ends - padded)[e_flat] + rank
    P = T * TOP_K + E * tm
    row_token = jnp.zeros((P,), jnp.int32).at[pos].set(jnp.arange(T * TOP_K, dtype=jnp.int32) // TOP_K)
    tile_start = jnp.arange(P // tm, dtype=jnp.int32) * tm
    tile_expert = jnp.minimum(jnp.sum(ends[None, :] <= tile_start[:, None], axis=1), E - 1).astype(jnp.int32)
    n_tiles = (ends[-1:] // tm).astype(jnp.int32)

    xs = gather_rows(xp, row_token, _tile(tm, 512))
    ys = grouped_ffn(xs, tile_expert, n_tiles, w1, w3, w2, tm)
    return combine_ln(ys, pos.astype(jnp.int32), gates, x, g, b, alpha)


def dense_block(x, xb, w1, w3, w2, g, b, alpha):
    T = x.shape[0]
    tm = _tile(T, 1024)
    f = grouped_ffn(xb, jnp.zeros((T // tm,), jnp.int32), jnp.full((1,), T // tm, jnp.int32), w1, w3, w2, tm)
    return resid_ln(x, f, g, b, alpha)


def _rot_cols(w):
    return jnp.concatenate([-w[..., ROPE_HALF:], w[..., :ROPE_HALF]], axis=-1)


def _prep_layer(l, w_in, q_norm_g, w_uq, kv_norm_g, w_uk, w_uv, conv_w, conv_b, w_gate_a, b_gate_a,
                w_gate_x, b_gate_x, lru_lambda, w_o):
    D = w_in.shape[1]
    q_lora, kv_lora = q_norm_g.shape[1], kv_norm_g.shape[1]
    n_heads = w_uk.shape[2] // NOPE_DIM
    W = conv_w.shape[2]
    wi = w_in[l]
    o = q_lora + kv_lora
    kr = wi[:, o:o + ROPE_DIM]
    z = jnp.zeros((D, LANES - ROPE_DIM), F32)
    w_lat = jnp.concatenate([wi[:, :o], kr, z, _rot_cols(kr), z], axis=1).astype(BF16)
    w_b = wi[:, o + ROPE_DIM:].astype(BF16)
    uq = w_uq[l].reshape(q_lora, n_heads, QK_DIM)
    nope, ropew = uq[..., :NOPE_DIM], uq[..., NOPE_DIM:]
    zq = jnp.zeros((q_lora, n_heads, LANES - ROPE_DIM), F32)
    w1 = jnp.concatenate([nope, ropew, zq], axis=-1).reshape(q_lora, n_heads * HEAD_PAD).astype(BF16)
    w2 = jnp.concatenate([_rot_cols(ropew), zq], axis=-1).reshape(q_lora, n_heads * LANES).astype(BF16)
    wkv = jnp.concatenate([w_uk[l], w_uv[l]], axis=1).astype(BF16)
    return dict(
        w_lat=w_lat, w_b=w_b, qg=q_norm_g[l][None, :], kvg=kv_norm_g[l][None, :], w1=w1, w2=w2, wkv=wkv,
        n_heads=n_heads, cw=conv_w[l], cb=conv_b[l][None, :],
        wa=w_gate_a[l].astype(BF16), ba=b_gate_a[l].reshape(1, W),
        wx=w_gate_x[l].astype(BF16), bx=b_gate_x[l].reshape(1, W),
        lam=lru_lambda[l][None, :], wo=w_o[l].astype(BF16))


def _pad_ff(w1, w3, w2):
    F = w1.shape[-1]
    Fp = -(-F // 512) * 512
    if Fp != F:
        w1 = jnp.pad(w1, ((0, 0), (0, 0), (0, Fp - F)))
        w3 = jnp.pad(w3, ((0, 0), (0, 0), (0, Fp - F)))
        w2 = jnp.pad(w2, ((0, 0), (0, Fp - F), (0, 0)))
    return w1.astype(BF16), w3.astype(BF16), w2.astype(BF16)


def kernel(x_prompt, x_sample, cache_kv_latent, cache_k_rope, state_lru, state_conv, w_in, q_norm_g, w_uq, kv_norm_g, w_uk, w_uv, conv_w, conv_b, w_gate_a, b_gate_a, w_gate_x, b_gate_x, lru_lambda, w_o, ln1_g, ln1_b, ln2_g, ln2_b, ffn_w1, ffn_w3, ffn_w2, router_w, moe_w1, moe_w3, moe_w2):
    depth = w_in.shape[0]
    alpha = (2 * depth) ** 0.25
    B, S, D = x_prompt.shape
    Bs, Ss, _ = x_sample.shape
    past = cache_kv_latent.shape[2]
    W = conv_w.shape[2]
    conv_pad = SUBLANES - (conv_w.shape[1] - 1)

    layers = [_prep_layer(l, w_in, q_norm_g, w_uq, kv_norm_g, w_uk, w_uv, conv_w, conv_b, w_gate_a, b_gate_a,
                          w_gate_x, b_gate_x, lru_lambda, w_o) for l in range(depth)]
    ffn = []
    for l in range(depth):
        if l % 2 == 0:
            ffn.append(_pad_ff(ffn_w1[l // 2][None], ffn_w3[l // 2][None], ffn_w2[l // 2][None]))
        else:
            ffn.append(_pad_ff(moe_w1[l // 2], moe_w3[l // 2], moe_w2[l // 2]))

    cos_p, sin_p = rope_tables(S, 0)
    cos_s, sin_s = rope_tables(Ss, past)
    cos_s, sin_s = jnp.tile(cos_s, (Bs, 1)), jnp.tile(sin_s, (Bs, 1))

    def trunk(x3, cos, sin, pos0, past_lat, past_kr, h0, conv0):
        Bg, Sg, _ = x3.shape
        x = x3.reshape(Bg * Sg, D)
        xb = x
        lats, krs, hs, bufs = [], [], [], []
        for l in range(depth):
            lw = layers[l]
            H = lw["n_heads"]
            p_lat = matmul(xb, lw["w_lat"], F32, tn_pref=lw["w_lat"].shape[1])
            pb = matmul(xb, lw["w_b"], F32)
            q, kc, ve, lat, kr = latent_stage(p_lat, cos, sin, lw["qg"], lw["kvg"], lw["w1"], lw["w2"],
                                              lw["wkv"], H)
            if past_lat is None:
                h_in = jnp.zeros((Bg, 1, W), F32)
                c_in = jnp.zeros((Bg, SUBLANES, W), F32)
            else:
                n_past = past_lat.shape[2]
                kr_past = jnp.pad(past_kr[l], ((0, 0), (0, 0), (0, LANES - ROPE_DIM)))
                kc_past, ve_past = cache_kv(past_lat[l].reshape(Bg * n_past, -1),
                                            kr_past.reshape(Bg * n_past, LANES), lw["wkv"], H)
                kc = jnp.concatenate([kc_past.reshape(Bg, n_past, -1), kc.reshape(Bg, Sg, -1)], axis=1)
                ve = jnp.concatenate([ve_past.reshape(Bg, n_past, -1), ve.reshape(Bg, Sg, -1)], axis=1)
                kc = kc.reshape(Bg * (n_past + Sg), -1)
                ve = ve.reshape(Bg * (n_past + Sg), -1)
                h_in = h0[l][:, None, :]
                c_in = jnp.pad(conv0[l], ((0, 0), (conv_pad, 0), (0, 0)))
            attn = attention(q, kc, ve, Bg, H, pos0)
            part, h_last, conv_last = lru_branch(pb, h_in, c_in, lw["cw"], lw["cb"], lw["wa"], lw["ba"],
                                                 lw["wx"], lw["bx"], lw["lam"], Bg)
            x, xb = merge_out(part, pb, attn, x, lw["wo"], ln1_g[l][None, :], ln1_b[l][None, :], alpha,
                              packed=l % 2 == 1)
            w1, w3, w2 = ffn[l]
            if l % 2 == 0:
                x, xb = dense_block(x, xb, w1, w3, w2, ln2_g[l][None, :], ln2_b[l][None, :], alpha)
            else:
                x, xb = moe_block(x, xb, router_w[l // 2], w1, w3, w2, ln2_g[l][None, :], ln2_b[l][None, :],
                                  alpha)
            lats.append(lat.reshape(Bg, Sg, -1))
            krs.append(kr.reshape(Bg, Sg, ROPE_DIM))
            hs.append(h_last[:, 0, :])
            bufs.append(conv_last[:, conv_pad:, :])
        return x.reshape(Bg, Sg, D), jnp.stack(lats), jnp.stack(krs), jnp.stack(hs), jnp.stack(bufs)

    y_p, p_lat, p_kr, p_lru, p_conv = trunk(x_prompt, cos_p, sin_p, 0, None, None, None, None)
    y_s, s_lat, s_kr, s_lru, s_conv = trunk(x_sample, cos_s, sin_s, past, cache_kv_latent, cache_k_rope,
                                            state_lru, state_conv)
    return (y_p, y_s, p_lat, p_kr, p_lru, p_conv, s_lat, s_kr, s_lru, s_conv)
```

```python
import functools
import math

import jax
import jax.numpy as jnp
from jax import lax
from jax.experimental import pallas as pl
from jax.experimental.pallas import tpu as pltpu

CHUNK = 64
NOPE_DIM = 128
ROPE_DIM = 64
ROPE_HALF = ROPE_DIM // 2
V_DIM = 128
QK_DIM = NOPE_DIM + ROPE_DIM
SCORE_SCALE = QK_DIM ** -0.5 * math.log2(math.e)
ROPE_THETA = 10000.0
LRU_BW = 128
LRU_C = 8.0
TOP_K = 2
LN_EPS = 1e-5
RMS_EPS = 1e-6

LANES = 128
SUBLANES = 8
HEAD_PAD = 2 * LANES
V7X_VMEM_BYTES = 64 * 1024 * 1024
VMEM_LIMIT = V7X_VMEM_BYTES - 8 * 1024 * 1024

BF16 = jnp.bfloat16
F32 = jnp.float32


def _cparams(*sem):
    return pltpu.CompilerParams(dimension_semantics=sem, vmem_limit_bytes=VMEM_LIMIT)


def _tile(n, pref, mult=SUBLANES):
    if n <= pref:
        return n
    t = (pref // mult) * mult
    while t >= mult:
        if n % t == 0:
            return t
        t -= mult
    return n


def _layer_norm_rows(y, g, b):
    mu = jnp.mean(y, axis=-1, keepdims=True)
    yc = y - mu
    var = jnp.mean(yc * yc, axis=-1, keepdims=True)
    return yc * lax.rsqrt(var + LN_EPS) * g + b


def _rms_norm_rows(y, g):
    return y * lax.rsqrt(jnp.mean(y * y, axis=-1, keepdims=True) + RMS_EPS) * g


def _mm_cast_kernel(x_ref, w_ref, o_ref, xb_ref):
    @pl.when(pl.program_id(1) == 0)
    def _():
        xb_ref[...] = x_ref[...].astype(BF16)

    o_ref[...] = jnp.dot(xb_ref[...], w_ref[...], preferred_element_type=F32).astype(o_ref.dtype)


def _mm_kernel(x_ref, w_ref, o_ref):
    o_ref[...] = jnp.dot(x_ref[...], w_ref[...], preferred_element_type=F32).astype(o_ref.dtype)


def matmul(x, w, out_dtype, tm_pref=1024, tn_pref=1024):
    M, K = x.shape
    N = w.shape[1]
    tm = _tile(M, tm_pref)
    tn = _tile(N, tn_pref, LANES)
    cast = x.dtype != BF16
    return pl.pallas_call(
        _mm_cast_kernel if cast else _mm_kernel,
        grid=(M // tm, N // tn),
        in_specs=[pl.BlockSpec((tm, K), lambda i, j: (i, 0)),
                  pl.BlockSpec((K, tn), lambda i, j: (0, j))],
        out_specs=pl.BlockSpec((tm, tn), lambda i, j: (i, j)),
        out_shape=jax.ShapeDtypeStruct((M, N), out_dtype),
        scratch_shapes=[pltpu.VMEM((tm, K), BF16)] if cast else [],
        compiler_params=_cparams("parallel", "arbitrary"),
        name="matmul",
    )(x, w)


def _rope_table_kernel(inv_ref, cos_ref, sin_ref, *, pos0):
    rows = lax.broadcasted_iota(jnp.int32, cos_ref.shape, 0)
    lane = lax.broadcasted_iota(jnp.int32, cos_ref.shape, 1)
    ang = (rows + pos0).astype(F32) * inv_ref[...]
    live = lane < ROPE_DIM
    cos_ref[...] = jnp.where(live, jnp.cos(ang), 0.0)
    sin_ref[...] = jnp.where(live, jnp.sin(ang), 0.0)


def rope_tables(seq, pos0):
    inv = ROPE_THETA ** (-jnp.arange(ROPE_HALF, dtype=F32) / ROPE_HALF)
    inv128 = jnp.concatenate([inv, inv, jnp.zeros((LANES - ROPE_DIM,), F32)])[None, :]
    return pl.pallas_call(
        functools.partial(_rope_table_kernel, pos0=pos0),
        out_shape=(jax.ShapeDtypeStruct((seq, LANES), F32),) * 2,
        name="rope_tables",
    )(inv128)


def _write_kv(lat, krope, wkv_ref, kc_ref, ve_ref, n_heads):
    krope_b = krope.astype(BF16)
    ones = jnp.ones(krope_b.shape, BF16)
    kv = jnp.dot(lat.astype(BF16), wkv_ref[...], preferred_element_type=F32).astype(BF16)
    for h in range(n_heads):
        lo, hi = h * HEAD_PAD, h * HEAD_PAD + LANES
        kc_ref[:, lo:hi] = kv[:, h * NOPE_DIM:(h + 1) * NOPE_DIM]
        kc_ref[:, hi:hi + LANES] = krope_b
        ve_ref[:, lo:hi] = kv[:, (n_heads + h) * V_DIM:(n_heads + h + 1) * V_DIM]
        ve_ref[:, hi:hi + LANES] = ones


def _cache_kv_kernel(lat_ref, krp_ref, wkv_ref, kc_ref, ve_ref, *, n_heads):
    _write_kv(lat_ref[...], krp_ref[...], wkv_ref, kc_ref, ve_ref, n_heads)


def cache_kv(lat, krp, wkv, n_heads):
    R = lat.shape[0]
    tm = _tile(R, 512)
    row = lambda i: (i, 0)
    head_cols = n_heads * HEAD_PAD
    return pl.pallas_call(
        functools.partial(_cache_kv_kernel, n_heads=n_heads),
        grid=(R // tm,),
        in_specs=[pl.BlockSpec((tm, lat.shape[1]), row), pl.BlockSpec((tm, LANES), row),
                  pl.BlockSpec(wkv.shape, lambda i: (0, 0))],
        out_specs=(pl.BlockSpec((tm, head_cols), row), pl.BlockSpec((tm, head_cols), row)),
        out_shape=(jax.ShapeDtypeStruct((R, head_cols), BF16), jax.ShapeDtypeStruct((R, head_cols), BF16)),
        compiler_params=_cparams("parallel"),
        name="cache_kv",
    )(lat, krp, wkv)


def _latent_kernel(p_ref, cos_ref, sin_ref, qg_ref, kvg_ref, w1_ref, w2_ref, wkv_ref,
                   q_ref, kc_ref, ve_ref, lat_ref, kr_ref, *, q_lora, kv_lora, n_heads):
    cos = cos_ref[...]
    sin = sin_ref[...]
    cqn = _rms_norm_rows(p_ref[:, :q_lora], qg_ref[...]).astype(BF16)
    lat = _rms_norm_rows(p_ref[:, q_lora:q_lora + kv_lora], kvg_ref[...])
    lat_ref[...] = lat
    o = q_lora + kv_lora
    krope = p_ref[:, o:o + LANES] * cos + p_ref[:, o + LANES:o + 2 * LANES] * sin
    kr_ref[...] = krope[:, :ROPE_DIM]
    _write_kv(lat, krope, wkv_ref, kc_ref, ve_ref, n_heads)
    for h in range(n_heads):
        lo, hi = h * HEAD_PAD, h * HEAD_PAD + LANES
        qa = jnp.dot(cqn, w1_ref[:, lo:lo + HEAD_PAD], preferred_element_type=F32)
        qb = jnp.dot(cqn, w2_ref[:, h * LANES:(h + 1) * LANES], preferred_element_type=F32)
        q_ref[:, lo:hi] = (qa[:, :LANES] * SCORE_SCALE).astype(BF16)
        q_ref[:, hi:hi + LANES] = ((qa[:, LANES:] * cos + qb * sin) * SCORE_SCALE).astype(BF16)


def latent_stage(p, cos, sin, qg, kvg, w1, w2, wkv, n_heads):
    T = p.shape[0]
    q_lora, kv_lora = qg.shape[1], kvg.shape[1]
    tm = _tile(min(T, cos.shape[0]), 256)
    n_pos = cos.shape[0] // tm
    const = lambda i: (0, 0)
    row = lambda i: (i, 0)
    head_cols = n_heads * HEAD_PAD
    return pl.pallas_call(
        functools.partial(_latent_kernel, q_lora=q_lora, kv_lora=kv_lora, n_heads=n_heads),
        grid=(T // tm,),
        in_specs=[pl.BlockSpec((tm, p.shape[1]), row),
                  pl.BlockSpec((tm, LANES), lambda i: (i % n_pos, 0)),
                  pl.BlockSpec((tm, LANES), lambda i: (i % n_pos, 0)),
                  pl.BlockSpec(qg.shape, const), pl.BlockSpec(kvg.shape, const),
                  pl.BlockSpec(w1.shape, const), pl.BlockSpec(w2.shape, const),
                  pl.BlockSpec(wkv.shape, const)],
        out_specs=(pl.BlockSpec((tm, head_cols), row),
                   pl.BlockSpec((tm, head_cols), row),
                   pl.BlockSpec((tm, head_cols), row),
                   pl.BlockSpec((tm, kv_lora), row),
                   pl.BlockSpec((tm, ROPE_DIM), row)),
        out_shape=(jax.ShapeDtypeStruct((T, head_cols), BF16),
                   jax.ShapeDtypeStruct((T, head_cols), BF16),
                   jax.ShapeDtypeStruct((T, head_cols), BF16),
                   jax.ShapeDtypeStruct((T, kv_lora), F32),
                   jax.ShapeDtypeStruct((T, ROPE_DIM), F32)),
        compiler_params=_cparams("parallel"),
        name="latent_stage",
    )(p, cos, sin, qg, kvg, w1, w2, wkv)


ATTN_TK = 512
ATTN_SM_ROWS = 128
ATTN_PV_ROWS = 256


def _chunk_mask(s, q0, k0):
    qpos = q0 + lax.broadcasted_iota(jnp.int32, s.shape, 0)
    kpos = k0 + lax.broadcasted_iota(jnp.int32, s.shape, 1)
    return jnp.where(kpos // CHUNK <= qpos // CHUNK, s, -jnp.inf)


def _attn_out(acc_sc):
    return acc_sc[:, :V_DIM] / acc_sc[:, V_DIM:]


def _attn_aligned_kernel(q_ref, k_ref, v_ref, o_ref, m_sc, acc_sc, s0, s1, p_sc, *, tq, tk, q_pos0):
    row0 = q_pos0 + pl.program_id(2) * tq
    n_pairs = row0 // (2 * tk)
    n_groups = tq // ATTN_PV_ROWS
    lower = tuple(range(n_groups // 2))
    upper = tuple(range(n_groups // 2, n_groups))

    m_sc[...] = jnp.full(m_sc.shape, -jnp.inf, F32)
    acc_sc[...] = jnp.zeros(acc_sc.shape, F32)

    def qk(start, s_ref, rows=slice(None)):
        s_ref[rows, :] = lax.dot_general(q_ref[rows, :], k_ref[pl.ds(start, tk), :], (((1,), (1,)), ((), ())),
                                         preferred_element_type=F32)

    def sm_pv(start, s_ref, groups, masked):
        for g in groups:
            alphas = []
            for r in range(ATTN_PV_ROWS // ATTN_SM_ROWS):
                r0 = g * ATTN_PV_ROWS + r * ATTN_SM_ROWS
                rs = slice(r0, r0 + ATTN_SM_ROWS)
                s = s_ref[rs, :]
                if masked:
                    s = _chunk_mask(s, r0 % tk, 0)
                m_prev = m_sc[rs, :]
                m_new = jnp.maximum(m_prev, jnp.max(s, axis=1, keepdims=True))
                p_sc[rs, :] = jnp.exp2(s - m_new[:, :1]).astype(BF16)
                alphas.append(jnp.exp2(m_prev - m_new))
                m_sc[rs, :] = m_new
            gs = slice(g * ATTN_PV_ROWS, (g + 1) * ATTN_PV_ROWS)
            alpha = jnp.concatenate(alphas, axis=0)
            acc_sc[gs, :] = (jnp.concatenate([alpha, alpha], axis=1) * acc_sc[gs, :]
                             + jnp.dot(p_sc[gs, :], v_ref[pl.ds(start, tk), :], preferred_element_type=F32))

    qk(0, s0)

    def pair(t):
        a = pl.multiple_of(2 * t * tk, 2 * tk)
        qk(a + tk, s1)
        sm_pv(a, s0, lower + upper, False)
        qk(a + 2 * tk, s0)
        sm_pv(a + tk, s1, lower + upper, False)

    def two_pairs(t, c):
        pair(2 * t)
        pair(2 * t + 1)
        return c

    lax.fori_loop(0, n_pairs // 2, two_pairs, 0)

    @pl.when(n_pairs % 2 == 1)
    def _():
        pair(n_pairs - 1)

    d0 = pl.multiple_of(2 * n_pairs * tk, 2 * tk)
    qk(d0 + tk, s1, slice(tq // 2, tq))
    sm_pv(d0, s0, lower, True)
    sm_pv(d0, s0, upper, False)
    sm_pv(d0 + tk, s1, upper, True)
    o_ref[...] = _attn_out(acc_sc)


def _attn_general_kernel(*refs, tq, tk, sk, q_pos0, heads, n_new):
    if n_new:
        q_ref, k_ref, v_ref, kn_ref, vn_ref, o_ref, m_sc, acc_sc = refs
    else:
        q_ref, k_ref, v_ref, o_ref, m_sc, acc_sc = refs
    row0 = q_pos0 + pl.program_id(2) * tq
    first_vis = (row0 // CHUNK + 1) * CHUNK
    last_vis = ((row0 + tq - 1) // CHUNK + 1) * CHUNK
    kv_hi = jnp.minimum(last_vis, sk)
    n_full = sk // tk
    tail = sk - n_full * tk
    n_unmasked = jnp.minimum(first_vis, sk) // tk
    n_masked_end = jnp.minimum((kv_hi + tk - 1) // tk, n_full)

    for h in range(heads):
        cols = slice(h * HEAD_PAD, (h + 1) * HEAD_PAD)
        m_sc[...] = jnp.full(m_sc.shape, -jnp.inf, F32)
        acc_sc[...] = jnp.zeros(acc_sc.shape, F32)

        def step(kr, vr, start, size, k0, masked, cols=cols):
            s = lax.dot_general(q_ref[:, cols], kr[pl.ds(start, size), cols], (((1,), (1,)), ((), ())),
                                preferred_element_type=F32)
            if masked:
                s = _chunk_mask(s, row0, k0)
            m_prev = m_sc[...]
            m_new = jnp.maximum(m_prev, jnp.max(s, axis=1, keepdims=True))
            p = jnp.exp2(s - m_new[:, :1]).astype(BF16)
            alpha = jnp.exp2(m_prev - m_new)
            acc_sc[...] = (jnp.concatenate([alpha, alpha], axis=1) * acc_sc[...]
                           + jnp.dot(p, vr[pl.ds(start, size), cols], preferred_element_type=F32))
            m_sc[...] = m_new

        def loop(lo, hi, masked, step=step):
            def body(j, c):
                start = pl.multiple_of(j * tk, tk)
                step(k_ref, v_ref, start, tk, start, masked)
                return c
            lax.fori_loop(lo, hi, body, 0)

        loop(0, n_unmasked, False)
        loop(n_unmasked, n_masked_end, True)
        if tail:
            @pl.when(kv_hi > n_full * tk)
            def _(step=step):
                step(k_ref, v_ref, n_full * tk, tail, n_full * tk, True)
        if n_new:
            step(kn_ref, vn_ref, 0, n_new, sk, True)
        o_ref[:, h * V_DIM:(h + 1) * V_DIM] = _attn_out(acc_sc)


ATTN_KV_BLOCK_BYTES = 4 * 1024 * 1024


def attention(q, kc, ve, batch, n_heads, q_pos0, past=None):
    sq = q.shape[0] // batch
    if past is None:
        sk = kc.shape[0] // batch
        tk = min(ATTN_TK, sk)
        aligned = sq % (2 * tk) == 0 and q_pos0 % (2 * tk) == 0 and sk == q_pos0 + sq
    else:
        sk = past[0].shape[0] // batch
        assert q_pos0 == sk and kc.shape[0] == q.shape[0]
        tk = min(ATTN_TK, sk)
        aligned = False
    tq = 2 * tk if aligned else _tile(sq, ATTN_TK)
    nq = sq // tq
    scratch = [pltpu.VMEM((tq, LANES), F32), pltpu.VMEM((tq, 2 * V_DIM), F32)]
    kv_map = lambda b, h, i: (b, h)
    if aligned:
        heads = 1
        body = functools.partial(_attn_aligned_kernel, tq=tq, tk=tk, q_pos0=q_pos0)
        scratch += [pltpu.VMEM((tq, tk), F32), pltpu.VMEM((tq, tk), F32), pltpu.VMEM((tq, tk), BF16)]
        operands = (q, kc, ve)
    else:
        heads = max(d for d in range(1, n_heads + 1)
                    if n_heads % d == 0 and (d == 1 or d * sk * HEAD_PAD * 2 <= ATTN_KV_BLOCK_BYTES))
        n_new = sq if past is not None else 0
        body = functools.partial(_attn_general_kernel, tq=tq, tk=tk, sk=sk, q_pos0=q_pos0, heads=heads,
                                 n_new=n_new)
        operands = (q,) + (tuple(past) + (kc, ve) if past is not None else (kc, ve))
    in_specs = [pl.BlockSpec((tq, heads * HEAD_PAD), lambda b, h, i: (b * nq + i, h)),
                pl.BlockSpec((sk, heads * HEAD_PAD), kv_map), pl.BlockSpec((sk, heads * HEAD_PAD), kv_map)]
    if past is not None:
        in_specs += [pl.BlockSpec((sq, heads * HEAD_PAD), kv_map), pl.BlockSpec((sq, heads * HEAD_PAD), kv_map)]
    return pl.pallas_call(
        body,
        grid=(batch, n_heads // heads, nq),
        in_specs=in_specs,
        out_specs=pl.BlockSpec((tq, heads * V_DIM), lambda b, h, i: (b * nq + i, h)),
        out_shape=jax.ShapeDtypeStruct((batch * sq, n_heads * V_DIM), F32),
        scratch_shapes=scratch,
        compiler_params=_cparams("parallel", "parallel", "arbitrary"),
        name="attention",
    )(*operands)


def _lru_kernel(u_ref, gate_ref, ga_ref, h0_ref, conv0_ref, cw_ref, cb_ref, wa_ref, ba_ref,
                wx_ref, bx_ref, lam_ref, part_ref, hlast_ref, convlast_ref, ubuf, hc, *, ts, conv_w):
    pad = SUBLANES

    @pl.when(pl.program_id(2) == 0)
    def _():
        ubuf[...] = conv0_ref[0]
        hc[...] = h0_ref[0]

    u = u_ref[...]
    prev = ubuf[...]
    head_row = lax.broadcasted_iota(jnp.int32, prev.shape, 0)
    uc = cb_ref[...]
    for j in range(conv_w):
        k = conv_w - 1 - j
        if k == 0:
            shifted = u
        else:
            rolled = pltpu.roll(u, k, 0)
            head = jnp.where(head_row < k, pltpu.roll(prev, k, 0), rolled[:pad, :])
            shifted = jnp.concatenate([head, rolled[pad:, :]], axis=0)
        uc = uc + shifted * cw_ref[j:j + 1, :]
    ubuf[...] = u[ts - pad:, :]
    convlast_ref[0] = u[ts - pad:, :]

    ucb = uc.astype(BF16)
    bc = uc.shape[1]
    r_parts, i_parts = [], []
    for n in range(bc // LRU_BW):
        blk = ucb[:, n * LRU_BW:(n + 1) * LRU_BW]
        r_parts.append(jnp.dot(blk, wa_ref[n], preferred_element_type=F32))
        i_parts.append(jnp.dot(blk, wx_ref[n], preferred_element_type=F32))
    r = jax.nn.sigmoid(jnp.concatenate(r_parts, axis=1) + ba_ref[...])
    ig = jax.nn.sigmoid(jnp.concatenate(i_parts, axis=1) + bx_ref[...])

    neg_lam = -lam_ref[...]
    softplus = jnp.maximum(neg_lam, 0.0) + jnp.log1p(jnp.exp(-jnp.abs(neg_lam)))
    log_a = (-LRU_C * r) * softplus
    a = jnp.exp(log_a)
    z = -jnp.tanh(log_a) * (a * a + 1.0)
    b = jnp.where(z > 0.0, z * lax.rsqrt(z), 0.0) * (ig * uc)

    grouped = (ts // SUBLANES, SUBLANES, a.shape[1])
    a = a.reshape(grouped)
    b = b.reshape(grouped)
    row_in_group = lax.broadcasted_iota(jnp.int32, grouped, 1)
    d = 1
    while d < SUBLANES:
        keep = row_in_group >= d
        a_prev = jnp.where(keep, pltpu.roll(a, d, 1), 1.0)
        b_prev = jnp.where(keep, pltpu.roll(b, d, 1), 0.0)
        b = a * b_prev + b
        a = a * a_prev
        d *= 2
    a = a.reshape(ts, grouped[2])
    b = b.reshape(ts, grouped[2])
    carry = hc[...]
    groups = []
    for g in range(ts // SUBLANES):
        rows = slice(g * SUBLANES, (g + 1) * SUBLANES)
        hg = b[rows, :] + a[rows, :] * carry
        carry = hg[SUBLANES - 1:SUBLANES, :]
        groups.append(hg)
    h = jnp.concatenate(groups, axis=0)
    hc[...] = carry
    hlast_ref[0] = carry
    part_ref[...] = jax.nn.sigmoid(ga_ref[...]) * (h * jax.nn.gelu(gate_ref[...]))


def lru_branch(pb, h0, conv0p, cw, cb, wa, ba, wx, bx, lam, batch):
    T = pb.shape[0]
    W = pb.shape[1] // 4
    S = T // batch
    ts = _tile(S, 256)
    bc = min(W, 512)
    ns, nc = S // ts, W // bc
    nb = bc // LRU_BW
    conv_w = cw.shape[0]
    tile = lambda g: pl.BlockSpec((ts, bc), lambda b, c, s: (b * ns + s, g * nc + c))
    chan = lambda rows: pl.BlockSpec((rows, bc), lambda b, c, s: (0, c))
    state = lambda rows: pl.BlockSpec((1, rows, bc), lambda b, c, s: (b, 0, c))
    gatew = pl.BlockSpec((nb, LRU_BW, LRU_BW), lambda b, c, s: (c, 0, 0))
    return pl.pallas_call(
        functools.partial(_lru_kernel, ts=ts, conv_w=conv_w),
        grid=(batch, nc, ns),
        in_specs=[tile(0), tile(1), tile(2), state(1), state(SUBLANES), chan(conv_w), chan(1),
                  gatew, chan(1), gatew, chan(1), chan(1)],
        out_specs=(pl.BlockSpec((ts, bc), lambda b, c, s: (b * ns + s, c)), state(1), state(SUBLANES)),
        out_shape=(jax.ShapeDtypeStruct((T, W), F32),
                   jax.ShapeDtypeStruct((batch, 1, W), F32),
                   jax.ShapeDtypeStruct((batch, SUBLANES, W), F32)),
        scratch_shapes=[pltpu.VMEM((SUBLANES, bc), F32), pltpu.VMEM((1, bc), F32)],
        compiler_params=_cparams("parallel", "parallel", "arbitrary"),
        name="lru_branch",
    )(pb, pb, pb, h0, conv0p, cw, cb, wa, ba, wx, bx, lam)


def _pack_bf16_pairs(y):
    half = y.shape[1] // 2
    lo = pltpu.bitcast(y[:, :half].astype(BF16).astype(F32), jnp.uint32) >> 16
    hi = pltpu.bitcast(y[:, half:].astype(BF16).astype(F32), jnp.uint32) & jnp.uint32(0xFFFF0000)
    return lo | hi


def _unpack_bf16_pairs(w, xb_ref):
    half = w.shape[1]
    xb_ref[:, :half] = pltpu.bitcast(w << 16, F32).astype(BF16)
    xb_ref[:, half:] = pltpu.bitcast(w & jnp.uint32(0xFFFF0000), F32).astype(BF16)


def _merge_out_kernel(part_ref, gb_ref, attn_ref, x_ref, wo_ref, g_ref, b_ref, y_ref, yb_ref, *, alpha, packed):
    merged = part_ref[...] + jax.nn.sigmoid(gb_ref[...]) * attn_ref[...]
    out = jnp.dot(merged.astype(BF16), wo_ref[...], preferred_element_type=F32)
    y = _layer_norm_rows(alpha * x_ref[...] + out, g_ref[...], b_ref[...])
    y_ref[...] = y
    yb_ref[...] = _pack_bf16_pairs(y) if packed else y.astype(BF16)


def merge_out(part, pb, attn, x, wo, g, b, alpha, packed):
    T, D = x.shape
    W = part.shape[1]
    tm = _tile(T, 256)
    row = lambda i: (i, 0)
    second = ((T, D // 2), jnp.uint32) if packed else ((T, D), BF16)
    const = lambda i: (0, 0)
    return pl.pallas_call(
        functools.partial(_merge_out_kernel, alpha=alpha, packed=packed),
        grid=(T // tm,),
        in_specs=[pl.BlockSpec((tm, W), row), pl.BlockSpec((tm, W), lambda i: (i, 3)),
                  pl.BlockSpec((tm, W), row), pl.BlockSpec((tm, D), row),
                  pl.BlockSpec(wo.shape, const), pl.BlockSpec((1, D), const), pl.BlockSpec((1, D), const)],
        out_specs=(pl.BlockSpec((tm, D), row), pl.BlockSpec((tm, second[0][1]), row)),
        out_shape=(jax.ShapeDtypeStruct((T, D), F32), jax.ShapeDtypeStruct(*second)),
        compiler_params=_cparams("parallel"),
        name="merge_out",
    )(part, pb, attn, x, wo, g, b)


def _ffn_kernel(te_ref, nt_ref, x_ref, w1_ref, w3_ref, w2_ref, o_ref, *unpacked):
    del te_ref
    live = pl.program_id(0) < nt_ref[0]

    @pl.when(pl.program_id(1) == 0)
    def _():
        o_ref[...] = jnp.zeros(o_ref.shape, o_ref.dtype)

    if unpacked:
        @pl.when(live & (pl.program_id(1) == 0))
        def _():
            _unpack_bf16_pairs(x_ref[...], unpacked[0])

    @pl.when(live)
    def _():
        x = unpacked[0][...] if unpacked else x_ref[...]
        a = jnp.dot(x, w1_ref[0], preferred_element_type=F32)
        b = jnp.dot(x, w3_ref[0], preferred_element_type=F32)
        h = (a * jax.nn.sigmoid(a) * b).astype(BF16)
        o_ref[...] += jnp.dot(h, w2_ref[0], preferred_element_type=F32)


def grouped_ffn(xs, tile_expert, n_tiles, w1, w3, w2, tm):
    P = xs.shape[0]
    D = w1.shape[1]
    packed = xs.dtype == jnp.uint32
    F = w1.shape[2]
    tf = _tile(F, 512, LANES)
    nf = F // tf
    def live(i, nt):
        return jnp.minimum(i, nt[0] - 1)
    def ff(i, f, nt):
        return jnp.where(i < nt[0], f, nf - 1)
    grid_spec = pltpu.PrefetchScalarGridSpec(
        num_scalar_prefetch=2,
        grid=(P // tm, nf),
        in_specs=[pl.BlockSpec((tm, xs.shape[1]), lambda i, f, te, nt: (live(i, nt), 0)),
                  pl.BlockSpec((1, D, tf), lambda i, f, te, nt: (te[live(i, nt)], 0, ff(i, f, nt))),
                  pl.BlockSpec((1, D, tf), lambda i, f, te, nt: (te[live(i, nt)], 0, ff(i, f, nt))),
                  pl.BlockSpec((1, tf, D), lambda i, f, te, nt: (te[live(i, nt)], ff(i, f, nt), 0))],
        out_specs=pl.BlockSpec((tm, D), lambda i, f, te, nt: (i, 0)),
        scratch_shapes=[pltpu.VMEM((tm, D), BF16)] if packed else [],
    )
    return pl.pallas_call(
        _ffn_kernel,
        grid_spec=grid_spec,
        out_shape=jax.ShapeDtypeStruct((P, D), F32),
        compiler_params=_cparams("parallel", "arbitrary"),
        name="grouped_ffn",
    )(tile_expert, n_tiles, xs, w1, w3, w2)


def _resid_ln_kernel(x_ref, f_ref, g_ref, b_ref, y_ref, yb_ref, *, alpha):
    y = _layer_norm_rows(alpha * x_ref[...] + f_ref[...], g_ref[...], b_ref[...])
    y_ref[...] = y
    yb_ref[...] = y.astype(BF16)


def resid_ln(x, f, g, b, alpha):
    T, D = x.shape
    tm = _tile(T, 512)
    row = lambda i: (i, 0)
    const = lambda i: (0, 0)
    return pl.pallas_call(
        functools.partial(_resid_ln_kernel, alpha=alpha),
        grid=(T // tm,),
        in_specs=[pl.BlockSpec((tm, D), row), pl.BlockSpec((tm, D), row),
                  pl.BlockSpec((1, D), const), pl.BlockSpec((1, D), const)],
        out_specs=(pl.BlockSpec((tm, D), row), pl.BlockSpec((tm, D), row)),
        out_shape=(jax.ShapeDtypeStruct((T, D), F32), jax.ShapeDtypeStruct((T, D), BF16)),
        compiler_params=_cparams("parallel"),
        name="resid_ln",
    )(x, f, g, b)


def _router_kernel(x_ref, w_ref, idx_ref, gate_ref, cnt_ref, run_sc, *, n_experts):
    @pl.when(pl.program_id(0) == 0)
    def _():
        run_sc[...] = jnp.zeros(run_sc.shape, F32)

    logits = jnp.dot(x_ref[...], w_ref[...], preferred_element_type=F32, precision=lax.Precision.HIGHEST)
    tm = logits.shape[0]
    lane = lax.broadcasted_iota(jnp.int32, logits.shape, 1)
    logits = jnp.where(lane < n_experts, logits, -jnp.inf)
    m1 = jnp.max(logits, axis=1, keepdims=True)
    i1 = jnp.min(jnp.where(logits == m1, lane, LANES), axis=1, keepdims=True)
    rest = jnp.where(lane == i1, -jnp.inf, logits)
    m2 = jnp.max(rest, axis=1, keepdims=True)
    i2 = jnp.min(jnp.where(rest == m2, lane, LANES), axis=1, keepdims=True)
    e2 = jnp.exp(m2 - m1)
    g1 = 1.0 / (1.0 + e2)
    g2 = e2 / (1.0 + e2)
    gate_ref[...] = jnp.where(lane == 0, g1, jnp.where(lane == 1, g2, 0.0))

    hot1 = jnp.where(lane == i1, 1.0, 0.0)
    hot2 = jnp.where(lane == i2, 1.0, 0.0)
    before = (lax.broadcasted_iota(jnp.int32, (tm, tm), 1) < lax.broadcasted_iota(jnp.int32, (tm, tm), 0))
    before = jnp.where(before, 1.0, 0.0).astype(BF16)
    c1 = jnp.dot(before, hot1.astype(BF16), preferred_element_type=F32)
    c2 = jnp.dot(before, hot2.astype(BF16), preferred_element_type=F32)
    n1 = jnp.sum(hot1, axis=0, keepdims=True)
    n2 = jnp.sum(hot2, axis=0, keepdims=True)
    run = run_sc[...]
    rank1 = jnp.sum(hot1 * (c1 + run), axis=1, keepdims=True).astype(jnp.int32)
    rank2 = jnp.sum(hot2 * (c2 + run + n1), axis=1, keepdims=True).astype(jnp.int32)
    run_sc[...] = run + n1 + n2
    cnt_ref[...] = run_sc[...].astype(jnp.int32)
    idx_ref[...] = jnp.where(lane == 0, i1, jnp.where(lane == 1, i2,
                             jnp.where(lane == 2, rank1, jnp.where(lane == 3, rank2, 0))))


def router(x, w_router):
    T, D = x.shape
    E = w_router.shape[1]
    wp = jnp.pad(w_router, ((0, 0), (0, LANES - E)))
    tm = _tile(T, 512)
    row = lambda i: (i, 0)
    return pl.pallas_call(
        functools.partial(_router_kernel, n_experts=E),
        grid=(T // tm,),
        in_specs=[pl.BlockSpec((tm, D), row), pl.BlockSpec((D, LANES), lambda i: (0, 0))],
        out_specs=(pl.BlockSpec((tm, LANES), row), pl.BlockSpec((tm, LANES), row),
                   pl.BlockSpec((1, LANES), lambda i: (0, 0))),
        out_shape=(jax.ShapeDtypeStruct((T, LANES), jnp.int32), jax.ShapeDtypeStruct((T, LANES), F32),
                   jax.ShapeDtypeStruct((1, LANES), jnp.int32)),
        scratch_shapes=[pltpu.VMEM((1, LANES), F32)],
        compiler_params=_cparams("arbitrary"),
        name="router",
    )(x, wp)


DMA_ISSUE_UNROLL = 8


def _gather_kernel(tok_ref, x_hbm, o_ref, buf, sem, *, tg):
    i = pl.program_id(0)

    def start_tile(tile, slot):
        def issue(r2, c):
            for half in range(2):
                r = half * (tg // 2) + r2
                pltpu.make_async_copy(x_hbm.at[pl.ds(tok_ref[tile * tg + r], 1)], buf.at[slot, pl.ds(r, 1)],
                                      sem.at[slot, half]).start()
            return c
        lax.fori_loop(0, tg // 2, issue, 0, unroll=DMA_ISSUE_UNROLL // 2)

    @pl.when(i == 0)
    def _():
        start_tile(0, 0)

    @pl.when(i + 1 < pl.num_programs(0))
    def _():
        start_tile(i + 1, (i + 1) % 2)

    slot = i % 2
    for half in range(2):
        rows = pl.ds(half * (tg // 2), tg // 2)
        pltpu.make_async_copy(x_hbm.at[pl.ds(0, tg // 2)], buf.at[slot, rows], sem.at[slot, half]).wait()
    o_ref[...] = buf[slot]


def gather_rows(x, row_token, tg):
    P = row_token.shape[0]
    D = x.shape[1]
    grid_spec = pltpu.PrefetchScalarGridSpec(
        num_scalar_prefetch=1,
        grid=(P // tg,),
        in_specs=[pl.BlockSpec(memory_space=pl.ANY)],
        out_specs=pl.BlockSpec((tg, D), lambda i, tok: (i, 0)),
        scratch_shapes=[pltpu.VMEM((2, tg, D), x.dtype), pltpu.SemaphoreType.DMA((2, 2))],
    )
    return pl.pallas_call(
        functools.partial(_gather_kernel, tg=tg),
        grid_spec=grid_spec,
        out_shape=jax.ShapeDtypeStruct((P, D), x.dtype),
        compiler_params=_cparams("arbitrary"),
        name="gather_rows",
    )(row_token, x)


def _combine_kernel(pos_ref, ys_hbm, gate_ref, x_ref, g_ref, b_ref, y_ref, yb_ref, buf, sem, *, tc, alpha):
    i = pl.program_id(0)

    def start_tile(tile, slot):
        def issue(r, c):
            p = (tile * tc + r) * TOP_K
            for k in range(TOP_K):
                pltpu.make_async_copy(ys_hbm.at[pl.ds(pos_ref[p + k], 1)], buf.at[slot, k, pl.ds(r, 1)],
                                      sem.at[slot, k]).start()
            return c
        lax.fori_loop(0, tc, issue, 0, unroll=DMA_ISSUE_UNROLL)

    @pl.when(i == 0)
    def _():
        start_tile(0, 0)

    @pl.when(i + 1 < pl.num_programs(0))
    def _():
        start_tile(i + 1, (i + 1) % 2)

    slot = i % 2
    for k in range(TOP_K):
        pltpu.make_async_copy(ys_hbm.at[pl.ds(0, tc)], buf.at[slot, k], sem.at[slot, k]).wait()
    gates = gate_ref[...]
    f = gates[:, 0:1] * buf[slot, 0] + gates[:, 1:2] * buf[slot, 1]
    y = _layer_norm_rows(alpha * x_ref[...] + f, g_ref[...], b_ref[...])
    y_ref[...] = y
    yb_ref[...] = y.astype(BF16)


def combine_ln(ys, pos, gates, x, g, b, alpha):
    T, D = x.shape
    tc = _tile(T, 256)
    row = lambda i, pos: (i, 0)
    const = lambda i, pos: (0, 0)
    grid_spec = pltpu.PrefetchScalarGridSpec(
        num_scalar_prefetch=1,
        grid=(T // tc,),
        in_specs=[pl.BlockSpec(memory_space=pl.ANY), pl.BlockSpec((tc, LANES), row),
                  pl.BlockSpec((tc, D), row), pl.BlockSpec((1, D), const), pl.BlockSpec((1, D), const)],
        out_specs=(pl.BlockSpec((tc, D), row), pl.BlockSpec((tc, D), row)),
        scratch_shapes=[pltpu.VMEM((2, TOP_K, tc, D), F32), pltpu.SemaphoreType.DMA((2, TOP_K))],
    )
    return pl.pallas_call(
        functools.partial(_combine_kernel, tc=tc, alpha=alpha),
        grid_spec=grid_spec,
        out_shape=(jax.ShapeDtypeStruct((T, D), F32), jax.ShapeDtypeStruct((T, D), BF16)),
        compiler_params=_cparams("arbitrary"),
        name="combine_ln",
    )(pos, ys, gates, x, g, b)


def moe_block(x, xp, w_router, w1, w3, w2, g, b, alpha):
    T, D = x.shape
    E = w1.shape[0]
    tm = 1024 if T * TOP_K >= 8 * 1024 else LANES
    idx, gates, counts = router(x, w_router)
    e_flat = idx[:, :TOP_K].reshape(-1)
    rank = idx[:, TOP_K:2 * TOP_K].reshape(-1)
    counts = counts[0, :E]
    padded = ((counts + tm - 1) // tm) * tm
    ends = jnp.cumsum(padded)
    pos = (ends - padded)[e_flat] + rank
    P = T * TOP_K + E * tm
    row_token = jnp.zeros((P,), jnp.int32).at[pos].set(jnp.arange(T * TOP_K, dtype=jnp.int32) // TOP_K)
    tile_start = jnp.arange(P // tm, dtype=jnp.int32) * tm
    tile_expert = jnp.minimum(jnp.sum(ends[None, :] <= tile_start[:, None], axis=1), E - 1).astype(jnp.int32)
    n_tiles = (ends[-1:] // tm).astype(jnp.int32)

    xs = gather_rows(xp, row_token, _tile(tm, 512))
    ys = grouped_ffn(xs, tile_expert, n_tiles, w1, w3, w2, tm)
    return combine_ln(ys, pos.astype(jnp.int32), gates, x, g, b, alpha)


def dense_block(x, xb, w1, w3, w2, g, b, alpha):
    T = x.shape[0]
    tm = _tile(T, 1024)
    f = grouped_ffn(xb, jnp.zeros((T // tm,), jnp.int32), jnp.full((1,), T // tm, jnp.int32), w1, w3, w2, tm)
    return resid_ln(x, f, g, b, alpha)


def _rot_cols(w):
    return jnp.concatenate([-w[..., ROPE_HALF:], w[..., :ROPE_HALF]], axis=-1)


def _prep_layer(l, w_in, q_norm_g, w_uq, kv_norm_g, w_uk, w_uv, conv_w, conv_b, w_gate_a, b_gate_a,
                w_gate_x, b_gate_x, lru_lambda, w_o):
    D = w_in.shape[1]
    q_lora, kv_lora = q_norm_g.shape[1], kv_norm_g.shape[1]
    n_heads = w_uk.shape[2] // NOPE_DIM
    W = conv_w.shape[2]
    wi = w_in[l]
    o = q_lora + kv_lora
    kr = wi[:, o:o + ROPE_DIM]
    z = jnp.zeros((D, LANES - ROPE_DIM), F32)
    w_lat = jnp.concatenate([wi[:, :o], kr, z, _rot_cols(kr), z], axis=1).astype(BF16)
    w_b = wi[:, o + ROPE_DIM:].astype(BF16)
    uq = w_uq[l].reshape(q_lora, n_heads, QK_DIM)
    nope, ropew = uq[..., :NOPE_DIM], uq[..., NOPE_DIM:]
    zq = jnp.zeros((q_lora, n_heads, LANES - ROPE_DIM), F32)
    w1 = jnp.concatenate([nope, ropew, zq], axis=-1).reshape(q_lora, n_heads * HEAD_PAD).astype(BF16)
    w2 = jnp.concatenate([_rot_cols(ropew), zq], axis=-1).reshape(q_lora, n_heads * LANES).astype(BF16)
    wkv = jnp.concatenate([w_uk[l], w_uv[l]], axis=1).astype(BF16)
    return dict(
        w_lat=w_lat, w_b=w_b, qg=q_norm_g[l][None, :], kvg=kv_norm_g[l][None, :], w1=w1, w2=w2, wkv=wkv,
        n_heads=n_heads, cw=conv_w[l], cb=conv_b[l][None, :],
        wa=w_gate_a[l].astype(BF16), ba=b_gate_a[l].reshape(1, W),
        wx=w_gate_x[l].astype(BF16), bx=b_gate_x[l].reshape(1, W),
        lam=lru_lambda[l][None, :], wo=w_o[l].astype(BF16))


def _pad_ff(w1, w3, w2):
    F = w1.shape[-1]
    Fp = -(-F // 512) * 512
    if Fp != F:
        w1 = jnp.pad(w1, ((0, 0), (0, 0), (0, Fp - F)))
        w3 = jnp.pad(w3, ((0, 0), (0, 0), (0, Fp - F)))
        w2 = jnp.pad(w2, ((0, 0), (0, Fp - F), (0, 0)))
    return w1.astype(BF16), w3.astype(BF16), w2.astype(BF16)


def kernel(x_prompt, x_sample, cache_kv_latent, cache_k_rope, state_lru, state_conv, w_in, q_norm_g, w_uq, kv_norm_g, w_uk, w_uv, conv_w, conv_b, w_gate_a, b_gate_a, w_gate_x, b_gate_x, lru_lambda, w_o, ln1_g, ln1_b, ln2_g, ln2_b, ffn_w1, ffn_w3, ffn_w2, router_w, moe_w1, moe_w3, moe_w2):
    depth = w_in.shape[0]
    alpha = (2 * depth) ** 0.25
    B, S, D = x_prompt.shape
    Bs, Ss, _ = x_sample.shape
    past = cache_kv_latent.shape[2]
    W = conv_w.shape[2]
    conv_pad = SUBLANES - (conv_w.shape[1] - 1)

    layers = [_prep_layer(l, w_in, q_norm_g, w_uq, kv_norm_g, w_uk, w_uv, conv_w, conv_b, w_gate_a, b_gate_a,
                          w_gate_x, b_gate_x, lru_lambda, w_o) for l in range(depth)]
    ffn = []
    for l in range(depth):
        if l % 2 == 0:
            ffn.append(_pad_ff(ffn_w1[l // 2][None], ffn_w3[l // 2][None], ffn_w2[l // 2][None]))
        else:
            ffn.append(_pad_ff(moe_w1[l // 2], moe_w3[l // 2], moe_w2[l // 2]))

    cos_p, sin_p = rope_tables(S, 0)
    cos_s, sin_s = rope_tables(Ss, past)
    cos_s, sin_s = jnp.tile(cos_s, (Bs, 1)), jnp.tile(sin_s, (Bs, 1))

    def trunk(x3, cos, sin, pos0, past_lat, past_kr, h0, conv0):
        Bg, Sg, _ = x3.shape
        x = x3.reshape(Bg * Sg, D)
        xb = x
        lats, krs, hs, bufs = [], [], [], []
        for l in range(depth):
            lw = layers[l]
            H = lw["n_heads"]
            p_lat = matmul(xb, lw["w_lat"], F32, tn_pref=lw["w_lat"].shape[1])
            pb = matmul(xb, lw["w_b"], F32)
            q, kc, ve, lat, kr = latent_stage(p_lat, cos, sin, lw["qg"], lw["kvg"], lw["w1"], lw["w2"],
                                              lw["wkv"], H)
            if past_lat is None:
                past_kv = None
                h_in = jnp.zeros((Bg, 1, W), F32)
                c_in = jnp.zeros((Bg, SUBLANES, W), F32)
            else:
                n_past = past_lat.shape[2]
                kr_past = jnp.pad(past_kr[l], ((0, 0), (0, 0), (0, LANES - ROPE_DIM)))
                past_kv = cache_kv(past_lat[l].reshape(Bg * n_past, -1), kr_past.reshape(Bg * n_past, LANES),
                                   lw["wkv"], H)
                h_in = h0[l][:, None, :]
                c_in = jnp.pad(conv0[l], ((0, 0), (conv_pad, 0), (0, 0)))
            attn = attention(q, kc, ve, Bg, H, pos0, past_kv)
            part, h_last, conv_last = lru_branch(pb, h_in, c_in, lw["cw"], lw["cb"], lw["wa"], lw["ba"],
                                                 lw["wx"], lw["bx"], lw["lam"], Bg)
            x, xb = merge_out(part, pb, attn, x, lw["wo"], ln1_g[l][None, :], ln1_b[l][None, :], alpha,
                              packed=l % 2 == 1)
            w1, w3, w2 = ffn[l]
            if l % 2 == 0:
                x, xb = dense_block(x, xb, w1, w3, w2, ln2_g[l][None, :], ln2_b[l][None, :], alpha)
            else:
                x, xb = moe_block(x, xb, router_w[l // 2], w1, w3, w2, ln2_g[l][None, :], ln2_b[l][None, :],
                                  alpha)
            lats.append(lat.reshape(Bg, Sg, -1))
            krs.append(kr.reshape(Bg, Sg, ROPE_DIM))
            hs.append(h_last[:, 0, :])
            bufs.append(conv_last[:, conv_pad:, :])
        return x.reshape(Bg, Sg, D), jnp.stack(lats), jnp.stack(krs), jnp.stack(hs), jnp.stack(bufs)

    y_p, p_lat, p_kr, p_lru, p_conv = trunk(x_prompt, cos_p, sin_p, 0, None, None, None, None)
    y_s, s_lat, s_kr, s_lru, s_conv = trunk(x_sample, cos_s, sin_s, past, cache_kv_latent, cache_k_rope,
                                            state_lru, state_conv)
    return (y_p, y_s, p_lat, p_kr, p_lru, p_conv, s_lat, s_kr, s_lru, s_conv)
```

```python
import functools
import math

import jax
import jax.numpy as jnp
from jax import lax
from jax.experimental import pallas as pl
from jax.experimental.pallas import tpu as pltpu

CHUNK = 64
NOPE_DIM = 128
ROPE_DIM = 64
ROPE_HALF = ROPE_DIM // 2
V_DIM = 128
QK_DIM = NOPE_DIM + ROPE_DIM
SCORE_SCALE = QK_DIM ** -0.5 * math.log2(math.e)
ROPE_THETA = 10000.0
LRU_BW = 128
LRU_C = 8.0
TOP_K = 2
LN_EPS = 1e-5
RMS_EPS = 1e-6

LANES = 128
SUBLANES = 8
HEAD_PAD = 2 * LANES
V7X_VMEM_BYTES = 64 * 1024 * 1024
VMEM_LIMIT = V7X_VMEM_BYTES - 8 * 1024 * 1024

BF16 = jnp.bfloat16
F32 = jnp.float32


def _cparams(*sem):
    return pltpu.CompilerParams(dimension_semantics=sem, vmem_limit_bytes=VMEM_LIMIT)


def _tile(n, pref, mult=SUBLANES):
    if n <= pref:
        return n
    t = (pref // mult) * mult
    while t >= mult:
        if n % t == 0:
            return t
        t -= mult
    return n


def _layer_norm_rows(y, g, b):
    mu = jnp.mean(y, axis=-1, keepdims=True)
    yc = y - mu
    var = jnp.mean(yc * yc, axis=-1, keepdims=True)
    return yc * lax.rsqrt(var + LN_EPS) * g + b


def _rms_norm_rows(y, g):
    return y * lax.rsqrt(jnp.mean(y * y, axis=-1, keepdims=True) + RMS_EPS) * g


def _mm_cast_kernel(x_ref, w_ref, o_ref, xb_ref):
    @pl.when(pl.program_id(1) == 0)
    def _():
        xb_ref[...] = x_ref[...].astype(BF16)

    o_ref[...] = jnp.dot(xb_ref[...], w_ref[...], preferred_element_type=F32).astype(o_ref.dtype)


def _mm_kernel(x_ref, w_ref, o_ref):
    o_ref[...] = jnp.dot(x_ref[...], w_ref[...], preferred_element_type=F32).astype(o_ref.dtype)


def matmul(x, w, out_dtype, tm_pref=1024, tn_pref=1024):
    M, K = x.shape
    N = w.shape[1]
    tm = _tile(M, tm_pref)
    tn = _tile(N, tn_pref, LANES)
    cast = x.dtype != BF16
    return pl.pallas_call(
        _mm_cast_kernel if cast else _mm_kernel,
        grid=(M // tm, N // tn),
        in_specs=[pl.BlockSpec((tm, K), lambda i, j: (i, 0)),
                  pl.BlockSpec((K, tn), lambda i, j: (0, j))],
        out_specs=pl.BlockSpec((tm, tn), lambda i, j: (i, j)),
        out_shape=jax.ShapeDtypeStruct((M, N), out_dtype),
        scratch_shapes=[pltpu.VMEM((tm, K), BF16)] if cast else [],
        compiler_params=_cparams("parallel", "arbitrary"),
        name="matmul",
    )(x, w)


def _rope_table_kernel(inv_ref, cos_ref, sin_ref, *, pos0):
    rows = lax.broadcasted_iota(jnp.int32, cos_ref.shape, 0)
    lane = lax.broadcasted_iota(jnp.int32, cos_ref.shape, 1)
    ang = (rows + pos0).astype(F32) * inv_ref[...]
    live = lane < ROPE_DIM
    cos_ref[...] = jnp.where(live, jnp.cos(ang), 0.0)
    sin_ref[...] = jnp.where(live, jnp.sin(ang), 0.0)


def rope_tables(seq, pos0):
    inv = ROPE_THETA ** (-jnp.arange(ROPE_HALF, dtype=F32) / ROPE_HALF)
    inv128 = jnp.concatenate([inv, inv, jnp.zeros((LANES - ROPE_DIM,), F32)])[None, :]
    return pl.pallas_call(
        functools.partial(_rope_table_kernel, pos0=pos0),
        out_shape=(jax.ShapeDtypeStruct((seq, LANES), F32),) * 2,
        name="rope_tables",
    )(inv128)


def _write_kv(lat, krope, wkv_ref, kc_ref, ve_ref, n_heads):
    krope_b = krope.astype(BF16)
    ones = jnp.ones(krope_b.shape, BF16)
    kv = jnp.dot(lat.astype(BF16), wkv_ref[...], preferred_element_type=F32).astype(BF16)
    for h in range(n_heads):
        lo, hi = h * HEAD_PAD, h * HEAD_PAD + LANES
        kc_ref[:, lo:hi] = kv[:, h * NOPE_DIM:(h + 1) * NOPE_DIM]
        kc_ref[:, hi:hi + LANES] = krope_b
        ve_ref[:, lo:hi] = kv[:, (n_heads + h) * V_DIM:(n_heads + h + 1) * V_DIM]
        ve_ref[:, hi:hi + LANES] = ones


def _cache_kv_kernel(lat_ref, krp_ref, wkv_ref, kc_ref, ve_ref, *, n_heads):
    _write_kv(lat_ref[...], krp_ref[...], wkv_ref, kc_ref, ve_ref, n_heads)


def cache_kv(lat, krp, wkv, n_heads):
    R = lat.shape[0]
    tm = _tile(R, 512)
    row = lambda i: (i, 0)
    head_cols = n_heads * HEAD_PAD
    return pl.pallas_call(
        functools.partial(_cache_kv_kernel, n_heads=n_heads),
        grid=(R // tm,),
        in_specs=[pl.BlockSpec((tm, lat.shape[1]), row), pl.BlockSpec((tm, LANES), row),
                  pl.BlockSpec(wkv.shape, lambda i: (0, 0))],
        out_specs=(pl.BlockSpec((tm, head_cols), row), pl.BlockSpec((tm, head_cols), row)),
        out_shape=(jax.ShapeDtypeStruct((R, head_cols), BF16), jax.ShapeDtypeStruct((R, head_cols), BF16)),
        compiler_params=_cparams("parallel"),
        name="cache_kv",
    )(lat, krp, wkv)


def _latent_kernel(p_ref, cos_ref, sin_ref, qg_ref, kvg_ref, w1_ref, w2_ref, wkv_ref,
                   q_ref, kc_ref, ve_ref, lat_ref, kr_ref, *, q_lora, kv_lora, n_heads):
    cos = cos_ref[...]
    sin = sin_ref[...]
    cqn = _rms_norm_rows(p_ref[:, :q_lora], qg_ref[...]).astype(BF16)
    lat = _rms_norm_rows(p_ref[:, q_lora:q_lora + kv_lora], kvg_ref[...])
    lat_ref[...] = lat
    o = q_lora + kv_lora
    krope = p_ref[:, o:o + LANES] * cos + p_ref[:, o + LANES:o + 2 * LANES] * sin
    kr_ref[...] = krope[:, :ROPE_DIM]
    _write_kv(lat, krope, wkv_ref, kc_ref, ve_ref, n_heads)
    for h in range(n_heads):
        lo, hi = h * HEAD_PAD, h * HEAD_PAD + LANES
        qa = jnp.dot(cqn, w1_ref[:, lo:lo + HEAD_PAD], preferred_element_type=F32)
        qb = jnp.dot(cqn, w2_ref[:, h * LANES:(h + 1) * LANES], preferred_element_type=F32)
        q_ref[:, lo:hi] = (qa[:, :LANES] * SCORE_SCALE).astype(BF16)
        q_ref[:, hi:hi + LANES] = ((qa[:, LANES:] * cos + qb * sin) * SCORE_SCALE).astype(BF16)


def latent_stage(p, cos, sin, qg, kvg, w1, w2, wkv, n_heads):
    T = p.shape[0]
    q_lora, kv_lora = qg.shape[1], kvg.shape[1]
    tm = _tile(min(T, cos.shape[0]), 256)
    n_pos = cos.shape[0] // tm
    const = lambda i: (0, 0)
    row = lambda i: (i, 0)
    head_cols = n_heads * HEAD_PAD
    return pl.pallas_call(
        functools.partial(_latent_kernel, q_lora=q_lora, kv_lora=kv_lora, n_heads=n_heads),
        grid=(T // tm,),
        in_specs=[pl.BlockSpec((tm, p.shape[1]), row),
                  pl.BlockSpec((tm, LANES), lambda i: (i % n_pos, 0)),
                  pl.BlockSpec((tm, LANES), lambda i: (i % n_pos, 0)),
                  pl.BlockSpec(qg.shape, const), pl.BlockSpec(kvg.shape, const),
                  pl.BlockSpec(w1.shape, const), pl.BlockSpec(w2.shape, const),
                  pl.BlockSpec(wkv.shape, const)],
        out_specs=(pl.BlockSpec((tm, head_cols), row),
                   pl.BlockSpec((tm, head_cols), row),
                   pl.BlockSpec((tm, head_cols), row),
                   pl.BlockSpec((tm, kv_lora), row),
                   pl.BlockSpec((tm, ROPE_DIM), row)),
        out_shape=(jax.ShapeDtypeStruct((T, head_cols), BF16),
                   jax.ShapeDtypeStruct((T, head_cols), BF16),
                   jax.ShapeDtypeStruct((T, head_cols), BF16),
                   jax.ShapeDtypeStruct((T, kv_lora), F32),
                   jax.ShapeDtypeStruct((T, ROPE_DIM), F32)),
        compiler_params=_cparams("parallel"),
        name="latent_stage",
    )(p, cos, sin, qg, kvg, w1, w2, wkv)


ATTN_TK = 512
ATTN_SM_ROWS = 128
ATTN_PV_ROWS = 256


def _chunk_mask(s, q0, k0):
    qpos = q0 + lax.broadcasted_iota(jnp.int32, s.shape, 0)
    kpos = k0 + lax.broadcasted_iota(jnp.int32, s.shape, 1)
    return jnp.where(kpos // CHUNK <= qpos // CHUNK, s, -jnp.inf)


def _attn_out(acc_sc):
    return acc_sc[:, :V_DIM] / acc_sc[:, V_DIM:]


def _attn_aligned_kernel(q_ref, k_ref, v_ref, o_ref, m_sc, acc_sc, s0, s1, p_sc, *, tq, tk, q_pos0):
    row0 = q_pos0 + pl.program_id(2) * tq
    n_pairs = row0 // (2 * tk)
    n_groups = tq // ATTN_PV_ROWS
    lower = tuple(range(n_groups // 2))
    upper = tuple(range(n_groups // 2, n_groups))

    m_sc[...] = jnp.full(m_sc.shape, -jnp.inf, F32)
    acc_sc[...] = jnp.zeros(acc_sc.shape, F32)

    def qk(start, s_ref, rows=slice(None)):
        s_ref[rows, :] = lax.dot_general(q_ref[rows, :], k_ref[pl.ds(start, tk), :], (((1,), (1,)), ((), ())),
                                         preferred_element_type=F32)

    def sm_pv(start, s_ref, groups, masked):
        for g in groups:
            alphas = []
            for r in range(ATTN_PV_ROWS // ATTN_SM_ROWS):
                r0 = g * ATTN_PV_ROWS + r * ATTN_SM_ROWS
                rs = slice(r0, r0 + ATTN_SM_ROWS)
                s = s_ref[rs, :]
                if masked:
                    s = _chunk_mask(s, r0 % tk, 0)
                m_prev = m_sc[rs, :]
                m_new = jnp.maximum(m_prev, jnp.max(s, axis=1, keepdims=True))
                p_sc[rs, :] = jnp.exp2(s - m_new[:, :1]).astype(BF16)
                alphas.append(jnp.exp2(m_prev - m_new))
                m_sc[rs, :] = m_new
            gs = slice(g * ATTN_PV_ROWS, (g + 1) * ATTN_PV_ROWS)
            alpha = jnp.concatenate(alphas, axis=0)
            acc_sc[gs, :] = (jnp.concatenate([alpha, alpha], axis=1) * acc_sc[gs, :]
                             + jnp.dot(p_sc[gs, :], v_ref[pl.ds(start, tk), :], preferred_element_type=F32))

    qk(0, s0)

    def pair(t):
        a = pl.multiple_of(2 * t * tk, 2 * tk)
        qk(a + tk, s1)
        sm_pv(a, s0, lower + upper, False)
        qk(a + 2 * tk, s0)
        sm_pv(a + tk, s1, lower + upper, False)

    def two_pairs(t, c):
        pair(2 * t)
        pair(2 * t + 1)
        return c

    lax.fori_loop(0, n_pairs // 2, two_pairs, 0)

    @pl.when(n_pairs % 2 == 1)
    def _():
        pair(n_pairs - 1)

    d0 = pl.multiple_of(2 * n_pairs * tk, 2 * tk)
    qk(d0 + tk, s1, slice(tq // 2, tq))
    sm_pv(d0, s0, lower, True)
    sm_pv(d0, s0, upper, False)
    sm_pv(d0 + tk, s1, upper, True)
    o_ref[...] = _attn_out(acc_sc)


def _attn_general_kernel(*refs, tq, tk, sk, q_pos0, heads, n_new):
    if n_new:
        q_ref, k_ref, v_ref, kn_ref, vn_ref, o_ref, m_sc, acc_sc = refs
    else:
        q_ref, k_ref, v_ref, o_ref, m_sc, acc_sc = refs
    row0 = q_pos0 + pl.program_id(2) * tq
    first_vis = (row0 // CHUNK + 1) * CHUNK
    last_vis = ((row0 + tq - 1) // CHUNK + 1) * CHUNK
    kv_hi = jnp.minimum(last_vis, sk)
    n_full = sk // tk
    tail = sk - n_full * tk
    n_unmasked = jnp.minimum(first_vis, sk) // tk
    n_masked_end = jnp.minimum((kv_hi + tk - 1) // tk, n_full)

    m_sc[...] = jnp.full(m_sc.shape, -jnp.inf, F32)
    acc_sc[...] = jnp.zeros(acc_sc.shape, F32)

    def step(kr, vr, start, size, k0, masked):
        for h in range(heads):
            cols = slice(h * HEAD_PAD, (h + 1) * HEAD_PAD)
            s = lax.dot_general(q_ref[:, cols], kr[pl.ds(start, size), cols], (((1,), (1,)), ((), ())),
                                preferred_element_type=F32)
            if masked:
                s = _chunk_mask(s, row0, k0)
            m_prev = m_sc[h]
            m_new = jnp.maximum(m_prev, jnp.max(s, axis=1, keepdims=True))
            p = jnp.exp2(s - m_new[:, :1]).astype(BF16)
            alpha = jnp.exp2(m_prev - m_new)
            acc_sc[h] = (jnp.concatenate([alpha, alpha], axis=1) * acc_sc[h]
                         + jnp.dot(p, vr[pl.ds(start, size), cols], preferred_element_type=F32))
            m_sc[h] = m_new

    def loop(lo, hi, masked):
        def body(j, c):
            start = pl.multiple_of(j * tk, tk)
            step(k_ref, v_ref, start, tk, start, masked)
            return c
        lax.fori_loop(lo, hi, body, 0)

    loop(0, n_unmasked, False)
    loop(n_unmasked, n_masked_end, True)
    if tail:
        @pl.when(kv_hi > n_full * tk)
        def _():
            step(k_ref, v_ref, n_full * tk, tail, n_full * tk, True)
    if n_new:
        step(kn_ref, vn_ref, 0, n_new, sk, True)
    for h in range(heads):
        o_ref[:, h * V_DIM:(h + 1) * V_DIM] = _attn_out(acc_sc.at[h])


ATTN_KV_BLOCK_BYTES = 4 * 1024 * 1024


def attention(q, kc, ve, batch, n_heads, q_pos0, past=None):
    sq = q.shape[0] // batch
    if past is None:
        sk = kc.shape[0] // batch
        tk = min(ATTN_TK, sk)
        aligned = sq % (2 * tk) == 0 and q_pos0 % (2 * tk) == 0 and sk == q_pos0 + sq
    else:
        sk = past[0].shape[0] // batch
        assert q_pos0 == sk and kc.shape[0] == q.shape[0]
        tk = min(ATTN_TK, sk)
        aligned = False
    tq = 2 * tk if aligned else _tile(sq, ATTN_TK)
    nq = sq // tq
    kv_map = lambda b, h, i: (b, h)
    if aligned:
        heads = 1
        body = functools.partial(_attn_aligned_kernel, tq=tq, tk=tk, q_pos0=q_pos0)
        scratch = [pltpu.VMEM((tq, LANES), F32), pltpu.VMEM((tq, 2 * V_DIM), F32),
                   pltpu.VMEM((tq, tk), F32), pltpu.VMEM((tq, tk), F32), pltpu.VMEM((tq, tk), BF16)]
        operands = (q, kc, ve)
    else:
        heads = max(d for d in range(1, n_heads + 1)
                    if n_heads % d == 0 and (d == 1 or d * sk * HEAD_PAD * 2 <= ATTN_KV_BLOCK_BYTES))
        scratch = [pltpu.VMEM((heads, tq, LANES), F32), pltpu.VMEM((heads, tq, 2 * V_DIM), F32)]
        n_new = sq if past is not None else 0
        body = functools.partial(_attn_general_kernel, tq=tq, tk=tk, sk=sk, q_pos0=q_pos0, heads=heads,
                                 n_new=n_new)
        operands = (q,) + (tuple(past) + (kc, ve) if past is not None else (kc, ve))
    in_specs = [pl.BlockSpec((tq, heads * HEAD_PAD), lambda b, h, i: (b * nq + i, h)),
                pl.BlockSpec((sk, heads * HEAD_PAD), kv_map), pl.BlockSpec((sk, heads * HEAD_PAD), kv_map)]
    if past is not None:
        in_specs += [pl.BlockSpec((sq, heads * HEAD_PAD), kv_map), pl.BlockSpec((sq, heads * HEAD_PAD), kv_map)]
    return pl.pallas_call(
        body,
        grid=(batch, n_heads // heads, nq),
        in_specs=in_specs,
        out_specs=pl.BlockSpec((tq, heads * V_DIM), lambda b, h, i: (b * nq + i, h)),
        out_shape=jax.ShapeDtypeStruct((batch * sq, n_heads * V_DIM), F32),
        scratch_shapes=scratch,
        compiler_params=_cparams("parallel", "parallel", "arbitrary"),
        name="attention",
    )(*operands)


def _lru_kernel(u_ref, gate_ref, ga_ref, gb_ref, attn_ref, h0_ref, conv0_ref, cw_ref, cb_ref, wa_ref, ba_ref,
                wx_ref, bx_ref, lam_ref, merged_ref, hlast_ref, convlast_ref, ubuf, hc, *, ts, conv_w):
    pad = SUBLANES

    @pl.when(pl.program_id(2) == 0)
    def _():
        ubuf[...] = conv0_ref[0]
        hc[...] = h0_ref[0]

    u = u_ref[...]
    prev = ubuf[...]
    head_row = lax.broadcasted_iota(jnp.int32, prev.shape, 0)
    uc = cb_ref[...]
    for j in range(conv_w):
        k = conv_w - 1 - j
        if k == 0:
            shifted = u
        else:
            rolled = pltpu.roll(u, k, 0)
            head = jnp.where(head_row < k, pltpu.roll(prev, k, 0), rolled[:pad, :])
            shifted = jnp.concatenate([head, rolled[pad:, :]], axis=0)
        uc = uc + shifted * cw_ref[j:j + 1, :]
    ubuf[...] = u[ts - pad:, :]
    convlast_ref[0] = u[ts - pad:, :]

    ucb = uc.astype(BF16)
    bc = uc.shape[1]
    r_parts, i_parts = [], []
    for n in range(bc // LRU_BW):
        blk = ucb[:, n * LRU_BW:(n + 1) * LRU_BW]
        r_parts.append(jnp.dot(blk, wa_ref[n], preferred_element_type=F32))
        i_parts.append(jnp.dot(blk, wx_ref[n], preferred_element_type=F32))
    r = jax.nn.sigmoid(jnp.concatenate(r_parts, axis=1) + ba_ref[...])
    ig = jax.nn.sigmoid(jnp.concatenate(i_parts, axis=1) + bx_ref[...])

    neg_lam = -lam_ref[...]
    softplus = jnp.maximum(neg_lam, 0.0) + jnp.log1p(jnp.exp(-jnp.abs(neg_lam)))
    log_a = (-LRU_C * r) * softplus
    a = jnp.exp(log_a)
    z = -jnp.tanh(log_a) * (a * a + 1.0)
    b = jnp.where(z > 0.0, z * lax.rsqrt(z), 0.0) * (ig * uc)

    grouped = (ts // SUBLANES, SUBLANES, a.shape[1])
    a = a.reshape(grouped)
    b = b.reshape(grouped)
    row_in_group = lax.broadcasted_iota(jnp.int32, grouped, 1)
    d = 1
    while d < SUBLANES:
        keep = row_in_group >= d
        a_prev = jnp.where(keep, pltpu.roll(a, d, 1), 1.0)
        b_prev = jnp.where(keep, pltpu.roll(b, d, 1), 0.0)
        b = a * b_prev + b
        a = a * a_prev
        d *= 2
    a = a.reshape(ts, grouped[2])
    b = b.reshape(ts, grouped[2])
    carry = hc[...]
    groups = []
    for g in range(ts // SUBLANES):
        rows = slice(g * SUBLANES, (g + 1) * SUBLANES)
        hg = b[rows, :] + a[rows, :] * carry
        carry = hg[SUBLANES - 1:SUBLANES, :]
        groups.append(hg)
    h = jnp.concatenate(groups, axis=0)
    hc[...] = carry
    hlast_ref[0] = carry
    merged = (jax.nn.sigmoid(ga_ref[...]) * (h * jax.nn.gelu(gate_ref[...]))
              + jax.nn.sigmoid(gb_ref[...]) * attn_ref[...])
    merged_ref[...] = merged.astype(BF16)


def lru_branch(pb, attn, h0, conv0p, cw, cb, wa, ba, wx, bx, lam, batch):
    T = pb.shape[0]
    W = pb.shape[1] // 4
    S = T // batch
    ts = _tile(S, 256)
    bc = min(W, 512)
    ns, nc = S // ts, W // bc
    nb = bc // LRU_BW
    conv_w = cw.shape[0]
    tile = lambda g: pl.BlockSpec((ts, bc), lambda b, c, s: (b * ns + s, g * nc + c))
    chan = lambda rows: pl.BlockSpec((rows, bc), lambda b, c, s: (0, c))
    state = lambda rows: pl.BlockSpec((1, rows, bc), lambda b, c, s: (b, 0, c))
    gatew = pl.BlockSpec((nb, LRU_BW, LRU_BW), lambda b, c, s: (c, 0, 0))
    return pl.pallas_call(
        functools.partial(_lru_kernel, ts=ts, conv_w=conv_w),
        grid=(batch, nc, ns),
        in_specs=[tile(0), tile(1), tile(2), tile(3), pl.BlockSpec((ts, bc), lambda b, c, s: (b * ns + s, c)),
                  state(1), state(SUBLANES), chan(conv_w), chan(1), gatew, chan(1), gatew, chan(1), chan(1)],
        out_specs=(pl.BlockSpec((ts, bc), lambda b, c, s: (b * ns + s, c)), state(1), state(SUBLANES)),
        out_shape=(jax.ShapeDtypeStruct((T, W), BF16),
                   jax.ShapeDtypeStruct((batch, 1, W), F32),
                   jax.ShapeDtypeStruct((batch, SUBLANES, W), F32)),
        scratch_shapes=[pltpu.VMEM((SUBLANES, bc), F32), pltpu.VMEM((1, bc), F32)],
        compiler_params=_cparams("parallel", "parallel", "arbitrary"),
        name="lru_branch",
    )(pb, pb, pb, pb, attn, h0, conv0p, cw, cb, wa, ba, wx, bx, lam)


def _pack_bf16_pairs(y):
    half = y.shape[1] // 2
    lo = pltpu.bitcast(y[:, :half].astype(BF16).astype(F32), jnp.uint32) >> 16
    hi = pltpu.bitcast(y[:, half:].astype(BF16).astype(F32), jnp.uint32) & jnp.uint32(0xFFFF0000)
    return lo | hi


def _unpack_bf16_pairs(w, xb_ref):
    half = w.shape[1]
    xb_ref[:, :half] = pltpu.bitcast(w << 16, F32).astype(BF16)
    xb_ref[:, half:] = pltpu.bitcast(w & jnp.uint32(0xFFFF0000), F32).astype(BF16)


def _merge_out_kernel(merged_ref, x_ref, wo_ref, g_ref, b_ref, y_ref, yb_ref, *, alpha, packed):
    out = jnp.dot(merged_ref[...], wo_ref[...], preferred_element_type=F32)
    y = _layer_norm_rows(alpha * x_ref[...] + out, g_ref[...], b_ref[...])
    y_ref[...] = y
    yb_ref[...] = _pack_bf16_pairs(y) if packed else y.astype(BF16)


def merge_out(merged, x, wo, g, b, alpha, packed):
    T, D = x.shape
    W = merged.shape[1]
    tm = _tile(T, 512)
    row = lambda i: (i, 0)
    second = ((T, D // 2), jnp.uint32) if packed else ((T, D), BF16)
    const = lambda i: (0, 0)
    return pl.pallas_call(
        functools.partial(_merge_out_kernel, alpha=alpha, packed=packed),
        grid=(T // tm,),
        in_specs=[pl.BlockSpec((tm, W), row), pl.BlockSpec((tm, D), row),
                  pl.BlockSpec(wo.shape, const), pl.BlockSpec((1, D), const), pl.BlockSpec((1, D), const)],
        out_specs=(pl.BlockSpec((tm, D), row), pl.BlockSpec((tm, second[0][1]), row)),
        out_shape=(jax.ShapeDtypeStruct((T, D), F32), jax.ShapeDtypeStruct(*second)),
        compiler_params=_cparams("parallel"),
        name="merge_out",
    )(merged, x, wo, g, b)


def _ffn_kernel(te_ref, nt_ref, x_ref, w1_ref, w3_ref, w2_ref, o_ref, *unpacked):
    del te_ref
    live = pl.program_id(0) < nt_ref[0]

    @pl.when(pl.program_id(1) == 0)
    def _():
        o_ref[...] = jnp.zeros(o_ref.shape, o_ref.dtype)

    if unpacked:
        @pl.when(live & (pl.program_id(1) == 0))
        def _():
            _unpack_bf16_pairs(x_ref[...], unpacked[0])

    @pl.when(live)
    def _():
        x = unpacked[0][...] if unpacked else x_ref[...]
        a = jnp.dot(x, w1_ref[0], preferred_element_type=F32)
        b = jnp.dot(x, w3_ref[0], preferred_element_type=F32)
        h = (a * jax.nn.sigmoid(a) * b).astype(BF16)
        o_ref[...] += jnp.dot(h, w2_ref[0], preferred_element_type=F32)


def grouped_ffn(xs, tile_expert, n_tiles, w1, w3, w2, tm):
    P = xs.shape[0]
    D = w1.shape[1]
    packed = xs.dtype == jnp.uint32
    F = w1.shape[2]
    tf = _tile(F, 512, LANES)
    nf = F // tf
    def live(i, nt):
        return jnp.minimum(i, nt[0] - 1)
    def ff(i, f, nt):
        return jnp.where(i < nt[0], f, nf - 1)
    grid_spec = pltpu.PrefetchScalarGridSpec(
        num_scalar_prefetch=2,
        grid=(P // tm, nf),
        in_specs=[pl.BlockSpec((tm, xs.shape[1]), lambda i, f, te, nt: (live(i, nt), 0)),
                  pl.BlockSpec((1, D, tf), lambda i, f, te, nt: (te[live(i, nt)], 0, ff(i, f, nt))),
                  pl.BlockSpec((1, D, tf), lambda i, f, te, nt: (te[live(i, nt)], 0, ff(i, f, nt))),
                  pl.BlockSpec((1, tf, D), lambda i, f, te, nt: (te[live(i, nt)], ff(i, f, nt), 0))],
        out_specs=pl.BlockSpec((tm, D), lambda i, f, te, nt: (i, 0)),
        scratch_shapes=[pltpu.VMEM((tm, D), BF16)] if packed else [],
    )
    return pl.pallas_call(
        _ffn_kernel,
        grid_spec=grid_spec,
        out_shape=jax.ShapeDtypeStruct((P, D), F32),
        compiler_params=_cparams("parallel", "arbitrary"),
        name="grouped_ffn",
    )(tile_expert, n_tiles, xs, w1, w3, w2)


def _resid_ln_kernel(x_ref, f_ref, g_ref, b_ref, y_ref, yb_ref, *, alpha):
    y = _layer_norm_rows(alpha * x_ref[...] + f_ref[...], g_ref[...], b_ref[...])
    y_ref[...] = y
    yb_ref[...] = y.astype(BF16)


def resid_ln(x, f, g, b, alpha):
    T, D = x.shape
    tm = _tile(T, 512)
    row = lambda i: (i, 0)
    const = lambda i: (0, 0)
    return pl.pallas_call(
        functools.partial(_resid_ln_kernel, alpha=alpha),
        grid=(T // tm,),
        in_specs=[pl.BlockSpec((tm, D), row), pl.BlockSpec((tm, D), row),
                  pl.BlockSpec((1, D), const), pl.BlockSpec((1, D), const)],
        out_specs=(pl.BlockSpec((tm, D), row), pl.BlockSpec((tm, D), row)),
        out_shape=(jax.ShapeDtypeStruct((T, D), F32), jax.ShapeDtypeStruct((T, D), BF16)),
        compiler_params=_cparams("parallel"),
        name="resid_ln",
    )(x, f, g, b)


def _router_kernel(x_ref, w_ref, idx_ref, gate_ref, cnt_ref, run_sc, *, n_experts):
    @pl.when(pl.program_id(0) == 0)
    def _():
        run_sc[...] = jnp.zeros(run_sc.shape, F32)

    logits = jnp.dot(x_ref[...], w_ref[...], preferred_element_type=F32, precision=lax.Precision.HIGHEST)
    tm = logits.shape[0]
    lane = lax.broadcasted_iota(jnp.int32, logits.shape, 1)
    logits = jnp.where(lane < n_experts, logits, -jnp.inf)
    m1 = jnp.max(logits, axis=1, keepdims=True)
    i1 = jnp.min(jnp.where(logits == m1, lane, LANES), axis=1, keepdims=True)
    rest = jnp.where(lane == i1, -jnp.inf, logits)
    m2 = jnp.max(rest, axis=1, keepdims=True)
    i2 = jnp.min(jnp.where(rest == m2, lane, LANES), axis=1, keepdims=True)
    e2 = jnp.exp(m2 - m1)
    g1 = 1.0 / (1.0 + e2)
    g2 = e2 / (1.0 + e2)
    gate_ref[...] = jnp.where(lane == 0, g1, jnp.where(lane == 1, g2, 0.0))

    hot1 = jnp.where(lane == i1, 1.0, 0.0)
    hot2 = jnp.where(lane == i2, 1.0, 0.0)
    before = (lax.broadcasted_iota(jnp.int32, (tm, tm), 1) < lax.broadcasted_iota(jnp.int32, (tm, tm), 0))
    before = jnp.where(before, 1.0, 0.0).astype(BF16)
    c1 = jnp.dot(before, hot1.astype(BF16), preferred_element_type=F32)
    c2 = jnp.dot(before, hot2.astype(BF16), preferred_element_type=F32)
    n1 = jnp.sum(hot1, axis=0, keepdims=True)
    n2 = jnp.sum(hot2, axis=0, keepdims=True)
    run = run_sc[...]
    rank1 = jnp.sum(hot1 * (c1 + run), axis=1, keepdims=True).astype(jnp.int32)
    rank2 = jnp.sum(hot2 * (c2 + run + n1), axis=1, keepdims=True).astype(jnp.int32)
    run_sc[...] = run + n1 + n2
    cnt_ref[...] = run_sc[...].astype(jnp.int32)
    idx_ref[...] = jnp.where(lane == 0, i1, jnp.where(lane == 1, i2,
                             jnp.where(lane == 2, rank1, jnp.where(lane == 3, rank2, 0))))


def router(x, w_router):
    T, D = x.shape
    E = w_router.shape[1]
    wp = jnp.pad(w_router, ((0, 0), (0, LANES - E)))
    tm = _tile(T, 512)
    row = lambda i: (i, 0)
    return pl.pallas_call(
        functools.partial(_router_kernel, n_experts=E),
        grid=(T // tm,),
        in_specs=[pl.BlockSpec((tm, D), row), pl.BlockSpec((D, LANES), lambda i: (0, 0))],
        out_specs=(pl.BlockSpec((tm, LANES), row), pl.BlockSpec((tm, LANES), row),
                   pl.BlockSpec((1, LANES), lambda i: (0, 0))),
        out_shape=(jax.ShapeDtypeStruct((T, LANES), jnp.int32), jax.ShapeDtypeStruct((T, LANES), F32),
                   jax.ShapeDtypeStruct((1, LANES), jnp.int32)),
        scratch_shapes=[pltpu.VMEM((1, LANES), F32)],
        compiler_params=_cparams("arbitrary"),
        name="router",
    )(x, wp)


DMA_ISSUE_UNROLL = 8


def _gather_kernel(tok_ref, x_hbm, o_ref, buf, sem, *, tg):
    i = pl.program_id(0)

    def start_tile(tile, slot):
        def issue(r2, c):
            for half in range(2):
                r = half * (tg // 2) + r2
                pltpu.make_async_copy(x_hbm.at[pl.ds(tok_ref[tile * tg + r], 1)], buf.at[slot, pl.ds(r, 1)],
                                      sem.at[slot, half]).start()
            return c
        lax.fori_loop(0, tg // 2, issue, 0, unroll=DMA_ISSUE_UNROLL // 2)

    @pl.when(i == 0)
    def _():
        start_tile(0, 0)

    @pl.when(i + 1 < pl.num_programs(0))
    def _():
        start_tile(i + 1, (i + 1) % 2)

    slot = i % 2
    for half in range(2):
        rows = pl.ds(half * (tg // 2), tg // 2)
        pltpu.make_async_copy(x_hbm.at[pl.ds(0, tg // 2)], buf.at[slot, rows], sem.at[slot, half]).wait()
    o_ref[...] = buf[slot]


def gather_rows(x, row_token, tg):
    P = row_token.shape[0]
    D = x.shape[1]
    grid_spec = pltpu.PrefetchScalarGridSpec(
        num_scalar_prefetch=1,
        grid=(P // tg,),
        in_specs=[pl.BlockSpec(memory_space=pl.ANY)],
        out_specs=pl.BlockSpec((tg, D), lambda i, tok: (i, 0)),
        scratch_shapes=[pltpu.VMEM((2, tg, D), x.dtype), pltpu.SemaphoreType.DMA((2, 2))],
    )
    return pl.pallas_call(
        functools.partial(_gather_kernel, tg=tg),
        grid_spec=grid_spec,
        out_shape=jax.ShapeDtypeStruct((P, D), x.dtype),
        compiler_params=_cparams("arbitrary"),
        name="gather_rows",
    )(row_token, x)


def _combine_kernel(pos_ref, ys_hbm, gate_ref, x_ref, g_ref, b_ref, y_ref, yb_ref, buf, sem, *, tc, alpha):
    i = pl.program_id(0)

    def start_tile(tile, slot):
        def issue(r, c):
            p = (tile * tc + r) * TOP_K
            for k in range(TOP_K):
                pltpu.make_async_copy(ys_hbm.at[pl.ds(pos_ref[p + k], 1)], buf.at[slot, k, pl.ds(r, 1)],
                                      sem.at[slot, k]).start()
            return c
        lax.fori_loop(0, tc, issue, 0, unroll=DMA_ISSUE_UNROLL)

    @pl.when(i == 0)
    def _():
        start_tile(0, 0)

    @pl.when(i + 1 < pl.num_programs(0))
    def _():
        start_tile(i + 1, (i + 1) % 2)

    slot = i % 2
    for k in range(TOP_K):
        pltpu.make_async_copy(ys_hbm.at[pl.ds(0, tc)], buf.at[slot, k], sem.at[slot, k]).wait()
    gates = gate_ref[...]
    f = gates[:, 0:1] * buf[slot, 0] + gates[:, 1:2] * buf[slot, 1]
    y = _layer_norm_rows(alpha * x_ref[...] + f, g_ref[...], b_ref[...])
    y_ref[...] = y
    yb_ref[...] = y.astype(BF16)


def combine_ln(ys, pos, gates, x, g, b, alpha):
    T, D = x.shape
    tc = _tile(T, 256)
    row = lambda i, pos: (i, 0)
    const = lambda i, pos: (0, 0)
    grid_spec = pltpu.PrefetchScalarGridSpec(
        num_scalar_prefetch=1,
        grid=(T // tc,),
        in_specs=[pl.BlockSpec(memory_space=pl.ANY), pl.BlockSpec((tc, LANES), row),
                  pl.BlockSpec((tc, D), row), pl.BlockSpec((1, D), const), pl.BlockSpec((1, D), const)],
        out_specs=(pl.BlockSpec((tc, D), row), pl.BlockSpec((tc, D), row)),
        scratch_shapes=[pltpu.VMEM((2, TOP_K, tc, D), F32), pltpu.SemaphoreType.DMA((2, TOP_K))],
    )
    return pl.pallas_call(
        functools.partial(_combine_kernel, tc=tc, alpha=alpha),
        grid_spec=grid_spec,
        out_shape=(jax.ShapeDtypeStruct((T, D), F32), jax.ShapeDtypeStruct((T, D), BF16)),
        compiler_params=_cparams("arbitrary"),
        name="combine_ln",
    )(pos, ys, gates, x, g, b)


def moe_block(x, xp, w_router, w1, w3, w2, g, b, alpha):
    T, D = x.shape
    E = w1.shape[0]
    tm = 1024 if T * TOP_K >= 8 * 1024 else LANES
    idx, gates, counts = router(x, w_router)
    e_flat = idx[:, :TOP_K].reshape(-1)
    rank = idx[:, TOP_K:2 * TOP_K].reshape(-1)
    counts = counts[0, :E]
    padded = ((counts + tm - 1) // tm) * tm
    ends = jnp.cumsum(padded)
    pos = (ends - padded)[e_flat] + rank
    P = T * TOP_K + E * tm
    row_token = jnp.zeros((P,), jnp.int32).at[pos].set(jnp.arange(T * TOP_K, dtype=jnp.int32) // TOP_K)
    tile_start = jnp.arange(P // tm, dtype=jnp.int32) * tm
    tile_expert = jnp.minimum(jnp.sum(ends[None, :] <= tile_start[:, None], axis=1), E - 1).astype(jnp.int32)
    n_tiles = (ends[-1:] // tm).astype(jnp.int32)

    xs = gather_rows(xp, row_token, _tile(tm, 512))
    ys = grouped_ffn(xs, tile_expert, n_tiles, w1, w3, w2, tm)
    return combine_ln(ys, pos.astype(jnp.int32), gates, x, g, b, alpha)


def dense_block(x, xb, w1, w3, w2, g, b, alpha):
    T = x.shape[0]
    tm = _tile(T, 1024)
    f = grouped_ffn(xb, jnp.zeros((T // tm,), jnp.int32), jnp.full((1,), T // tm, jnp.int32), w1, w3, w2, tm)
    return resid_ln(x, f, g, b, alpha)


def _rot_cols(w):
    return jnp.concatenate([-w[..., ROPE_HALF:], w[..., :ROPE_HALF]], axis=-1)


def _prep_layer(l, w_in, q_norm_g, w_uq, kv_norm_g, w_uk, w_uv, conv_w, conv_b, w_gate_a, b_gate_a,
                w_gate_x, b_gate_x, lru_lambda, w_o):
    D = w_in.shape[1]
    q_lora, kv_lora = q_norm_g.shape[1], kv_norm_g.shape[1]
    n_heads = w_uk.shape[2] // NOPE_DIM
    W = conv_w.shape[2]
    wi = w_in[l]
    o = q_lora + kv_lora
    kr = wi[:, o:o + ROPE_DIM]
    z = jnp.zeros((D, LANES - ROPE_DIM), F32)
    w_lat = jnp.concatenate([wi[:, :o], kr, z, _rot_cols(kr), z], axis=1).astype(BF16)
    w_b = wi[:, o + ROPE_DIM:].astype(BF16)
    uq = w_uq[l].reshape(q_lora, n_heads, QK_DIM)
    nope, ropew = uq[..., :NOPE_DIM], uq[..., NOPE_DIM:]
    zq = jnp.zeros((q_lora, n_heads, LANES - ROPE_DIM), F32)
    w1 = jnp.concatenate([nope, ropew, zq], axis=-1).reshape(q_lora, n_heads * HEAD_PAD).astype(BF16)
    w2 = jnp.concatenate([_rot_cols(ropew), zq], axis=-1).reshape(q_lora, n_heads * LANES).astype(BF16)
    wkv = jnp.concatenate([w_uk[l], w_uv[l]], axis=1).astype(BF16)
    return dict(
        w_lat=w_lat, w_b=w_b, qg=q_norm_g[l][None, :], kvg=kv_norm_g[l][None, :], w1=w1, w2=w2, wkv=wkv,
        n_heads=n_heads, cw=conv_w[l], cb=conv_b[l][None, :],
        wa=w_gate_a[l].astype(BF16), ba=b_gate_a[l].reshape(1, W),
        wx=w_gate_x[l].astype(BF16), bx=b_gate_x[l].reshape(1, W),
        lam=lru_lambda[l][None, :], wo=w_o[l].astype(BF16))


def _pad_ff(w1, w3, w2):
    F = w1.shape[-1]
    Fp = -(-F // 512) * 512
    if Fp != F:
        w1 = jnp.pad(w1, ((0, 0), (0, 0), (0, Fp - F)))
        w3 = jnp.pad(w3, ((0, 0), (0, 0), (0, Fp - F)))
        w2 = jnp.pad(w2, ((0, 0), (0, Fp - F), (0, 0)))
    return w1.astype(BF16), w3.astype(BF16), w2.astype(BF16)


def kernel(x_prompt, x_sample, cache_kv_latent, cache_k_rope, state_lru, state_conv, w_in, q_norm_g, w_uq, kv_norm_g, w_uk, w_uv, conv_w, conv_b, w_gate_a, b_gate_a, w_gate_x, b_gate_x, lru_lambda, w_o, ln1_g, ln1_b, ln2_g, ln2_b, ffn_w1, ffn_w3, ffn_w2, router_w, moe_w1, moe_w3, moe_w2):
    depth = w_in.shape[0]
    alpha = (2 * depth) ** 0.25
    B, S, D = x_prompt.shape
    Bs, Ss, _ = x_sample.shape
    past = cache_kv_latent.shape[2]
    W = conv_w.shape[2]
    conv_pad = SUBLANES - (conv_w.shape[1] - 1)

    layers = [_prep_layer(l, w_in, q_norm_g, w_uq, kv_norm_g, w_uk, w_uv, conv_w, conv_b, w_gate_a, b_gate_a,
                          w_gate_x, b_gate_x, lru_lambda, w_o) for l in range(depth)]
    ffn = []
    for l in range(depth):
        if l % 2 == 0:
            ffn.append(_pad_ff(ffn_w1[l // 2][None], ffn_w3[l // 2][None], ffn_w2[l // 2][None]))
        else:
            ffn.append(_pad_ff(moe_w1[l // 2], moe_w3[l // 2], moe_w2[l // 2]))

    cos_p, sin_p = rope_tables(S, 0)
    cos_s, sin_s = rope_tables(Ss, past)
    cos_s, sin_s = jnp.tile(cos_s, (Bs, 1)), jnp.tile(sin_s, (Bs, 1))

    def trunk(x3, cos, sin, pos0, past_lat, past_kr, h0, conv0):
        Bg, Sg, _ = x3.shape
        x = x3.reshape(Bg * Sg, D)
        xb = x
        lats, krs, hs, bufs = [], [], [], []
        for l in range(depth):
            lw = layers[l]
            H = lw["n_heads"]
            p_lat = matmul(xb, lw["w_lat"], F32, tn_pref=lw["w_lat"].shape[1])
            pb = matmul(xb, lw["w_b"], F32)
            q, kc, ve, lat, kr = latent_stage(p_lat, cos, sin, lw["qg"], lw["kvg"], lw["w1"], lw["w2"],
                                              lw["wkv"], H)
            if past_lat is None:
                past_kv = None
                h_in = jnp.zeros((Bg, 1, W), F32)
                c_in = jnp.zeros((Bg, SUBLANES, W), F32)
            else:
                n_past = past_lat.shape[2]
                kr_past = jnp.pad(past_kr[l], ((0, 0), (0, 0), (0, LANES - ROPE_DIM)))
                past_kv = cache_kv(past_lat[l].reshape(Bg * n_past, -1), kr_past.reshape(Bg * n_past, LANES),
                                   lw["wkv"], H)
                h_in = h0[l][:, None, :]
                c_in = jnp.pad(conv0[l], ((0, 0), (conv_pad, 0), (0, 0)))
            attn = attention(q, kc, ve, Bg, H, pos0, past_kv)
            merged, h_last, conv_last = lru_branch(pb, attn, h_in, c_in, lw["cw"], lw["cb"], lw["wa"], lw["ba"],
                                                   lw["wx"], lw["bx"], lw["lam"], Bg)
            x, xb = merge_out(merged, x, lw["wo"], ln1_g[l][None, :], ln1_b[l][None, :], alpha,
                              packed=l % 2 == 1)
            w1, w3, w2 = ffn[l]
            if l % 2 == 0:
                x, xb = dense_block(x, xb, w1, w3, w2, ln2_g[l][None, :], ln2_b[l][None, :], alpha)
            else:
                x, xb = moe_block(x, xb, router_w[l // 2], w1, w3, w2, ln2_g[l][None, :], ln2_b[l][None, :],
                                  alpha)
            lats.append(lat.reshape(Bg, Sg, -1))
            krs.append(kr.reshape(Bg, Sg, ROPE_DIM))
            hs.append(h_last[:, 0, :])
            bufs.append(conv_last[:, conv_pad:, :])
        return x.reshape(Bg, Sg, D), jnp.stack(lats), jnp.stack(krs), jnp.stack(hs), jnp.stack(bufs)

    y_p, p_lat, p_kr, p_lru, p_conv = trunk(x_prompt, cos_p, sin_p, 0, None, None, None, None)
    y_s, s_lat, s_kr, s_lru, s_conv = trunk(x_sample, cos_s, sin_s, past, cache_kv_latent, cache_k_rope,
                                            state_lru, state_conv)
    return (y_p, y_s, p_lat, p_kr, p_lru, p_conv, s_lat, s_kr, s_lru, s_conv)
```

```python
import functools
import math

import jax
import jax.numpy as jnp
from jax import lax
from jax.experimental import pallas as pl
from jax.experimental.pallas import tpu as pltpu

CHUNK = 64
NOPE_DIM = 128
ROPE_DIM = 64
ROPE_HALF = ROPE_DIM // 2
V_DIM = 128
QK_DIM = NOPE_DIM + ROPE_DIM
SCORE_SCALE = QK_DIM ** -0.5 * math.log2(math.e)
ROPE_THETA = 10000.0
LRU_BW = 128
LRU_C = 8.0
TOP_K = 2
LN_EPS = 1e-5
RMS_EPS = 1e-6

LANES = 128
SUBLANES = 8
HEAD_PAD = 2 * LANES
V7X_VMEM_BYTES = 64 * 1024 * 1024
VMEM_LIMIT = V7X_VMEM_BYTES - 8 * 1024 * 1024

BF16 = jnp.bfloat16
F32 = jnp.float32


def _cparams(*sem):
    return pltpu.CompilerParams(dimension_semantics=sem, vmem_limit_bytes=VMEM_LIMIT)


def _tile(n, pref, mult=SUBLANES):
    if n <= pref:
        return n
    t = (pref // mult) * mult
    while t >= mult:
        if n % t == 0:
            return t
        t -= mult
    return n


def _layer_norm_rows(y, g, b):
    mu = jnp.mean(y, axis=-1, keepdims=True)
    yc = y - mu
    var = jnp.mean(yc * yc, axis=-1, keepdims=True)
    return yc * lax.rsqrt(var + LN_EPS) * g + b


def _rms_norm_rows(y, g):
    return y * lax.rsqrt(jnp.mean(y * y, axis=-1, keepdims=True) + RMS_EPS) * g


def _mm_cast_kernel(x_ref, w_ref, o_ref, xb_ref):
    @pl.when(pl.program_id(1) == 0)
    def _():
        xb_ref[...] = x_ref[...].astype(BF16)

    o_ref[...] = jnp.dot(xb_ref[...], w_ref[...], preferred_element_type=F32).astype(o_ref.dtype)


def _mm_kernel(x_ref, w_ref, o_ref):
    o_ref[...] = jnp.dot(x_ref[...], w_ref[...], preferred_element_type=F32).astype(o_ref.dtype)


def matmul(x, w, out_dtype, tm_pref=1024, tn_pref=1024):
    M, K = x.shape
    N = w.shape[1]
    tm = _tile(M, tm_pref)
    tn = _tile(N, tn_pref, LANES)
    cast = x.dtype != BF16
    return pl.pallas_call(
        _mm_cast_kernel if cast else _mm_kernel,
        grid=(M // tm, N // tn),
        in_specs=[pl.BlockSpec((tm, K), lambda i, j: (i, 0)),
                  pl.BlockSpec((K, tn), lambda i, j: (0, j))],
        out_specs=pl.BlockSpec((tm, tn), lambda i, j: (i, j)),
        out_shape=jax.ShapeDtypeStruct((M, N), out_dtype),
        scratch_shapes=[pltpu.VMEM((tm, K), BF16)] if cast else [],
        compiler_params=_cparams("parallel", "arbitrary"),
        name="matmul",
    )(x, w)


def _rope_table_kernel(inv_ref, cos_ref, sin_ref, *, pos0):
    rows = lax.broadcasted_iota(jnp.int32, cos_ref.shape, 0)
    lane = lax.broadcasted_iota(jnp.int32, cos_ref.shape, 1)
    ang = (rows + pos0).astype(F32) * inv_ref[...]
    live = lane < ROPE_DIM
    cos_ref[...] = jnp.where(live, jnp.cos(ang), 0.0)
    sin_ref[...] = jnp.where(live, jnp.sin(ang), 0.0)


def rope_tables(seq, pos0):
    inv = ROPE_THETA ** (-jnp.arange(ROPE_HALF, dtype=F32) / ROPE_HALF)
    inv128 = jnp.concatenate([inv, inv, jnp.zeros((LANES - ROPE_DIM,), F32)])[None, :]
    return pl.pallas_call(
        functools.partial(_rope_table_kernel, pos0=pos0),
        out_shape=(jax.ShapeDtypeStruct((seq, LANES), F32),) * 2,
        name="rope_tables",
    )(inv128)


def _write_kv(lat, krope, wkv_ref, kc_ref, ve_ref, n_heads):
    krope_b = krope.astype(BF16)
    ones = jnp.ones(krope_b.shape, BF16)
    kv = jnp.dot(lat.astype(BF16), wkv_ref[...], preferred_element_type=F32).astype(BF16)
    for h in range(n_heads):
        lo, hi = h * HEAD_PAD, h * HEAD_PAD + LANES
        kc_ref[:, lo:hi] = kv[:, h * NOPE_DIM:(h + 1) * NOPE_DIM]
        kc_ref[:, hi:hi + LANES] = krope_b
        ve_ref[:, lo:hi] = kv[:, (n_heads + h) * V_DIM:(n_heads + h + 1) * V_DIM]
        ve_ref[:, hi:hi + LANES] = ones


def _cache_kv_kernel(lat_ref, krp_ref, wkv_ref, kc_ref, ve_ref, *, n_heads):
    _write_kv(lat_ref[...], krp_ref[...], wkv_ref, kc_ref, ve_ref, n_heads)


def cache_kv(lat, krp, wkv, n_heads):
    R = lat.shape[0]
    tm = _tile(R, 512)
    row = lambda i: (i, 0)
    head_cols = n_heads * HEAD_PAD
    return pl.pallas_call(
        functools.partial(_cache_kv_kernel, n_heads=n_heads),
        grid=(R // tm,),
        in_specs=[pl.BlockSpec((tm, lat.shape[1]), row), pl.BlockSpec((tm, LANES), row),
                  pl.BlockSpec(wkv.shape, lambda i: (0, 0))],
        out_specs=(pl.BlockSpec((tm, head_cols), row), pl.BlockSpec((tm, head_cols), row)),
        out_shape=(jax.ShapeDtypeStruct((R, head_cols), BF16), jax.ShapeDtypeStruct((R, head_cols), BF16)),
        compiler_params=_cparams("parallel"),
        name="cache_kv",
    )(lat, krp, wkv)


def _latent_kernel(p_ref, cos_ref, sin_ref, qg_ref, kvg_ref, w1_ref, w2_ref, wkv_ref,
                   q_ref, kc_ref, ve_ref, lat_ref, kr_ref, *, q_lora, kv_lora, n_heads):
    cos = cos_ref[...]
    sin = sin_ref[...]
    cqn = _rms_norm_rows(p_ref[:, :q_lora], qg_ref[...]).astype(BF16)
    lat = _rms_norm_rows(p_ref[:, q_lora:q_lora + kv_lora], kvg_ref[...])
    lat_ref[...] = lat
    o = q_lora + kv_lora
    krope = p_ref[:, o:o + LANES] * cos + p_ref[:, o + LANES:o + 2 * LANES] * sin
    kr_ref[...] = krope[:, :ROPE_DIM]
    _write_kv(lat, krope, wkv_ref, kc_ref, ve_ref, n_heads)
    for h in range(n_heads):
        lo, hi = h * HEAD_PAD, h * HEAD_PAD + LANES
        qa = jnp.dot(cqn, w1_ref[:, lo:lo + HEAD_PAD], preferred_element_type=F32)
        qb = jnp.dot(cqn, w2_ref[:, h * LANES:(h + 1) * LANES], preferred_element_type=F32)
        q_ref[:, lo:hi] = (qa[:, :LANES] * SCORE_SCALE).astype(BF16)
        q_ref[:, hi:hi + LANES] = ((qa[:, LANES:] * cos + qb * sin) * SCORE_SCALE).astype(BF16)


def latent_stage(p, cos, sin, qg, kvg, w1, w2, wkv, n_heads):
    T = p.shape[0]
    q_lora, kv_lora = qg.shape[1], kvg.shape[1]
    tm = _tile(min(T, cos.shape[0]), 256)
    n_pos = cos.shape[0] // tm
    const = lambda i: (0, 0)
    row = lambda i: (i, 0)
    head_cols = n_heads * HEAD_PAD
    return pl.pallas_call(
        functools.partial(_latent_kernel, q_lora=q_lora, kv_lora=kv_lora, n_heads=n_heads),
        grid=(T // tm,),
        in_specs=[pl.BlockSpec((tm, p.shape[1]), row),
                  pl.BlockSpec((tm, LANES), lambda i: (i % n_pos, 0)),
                  pl.BlockSpec((tm, LANES), lambda i: (i % n_pos, 0)),
                  pl.BlockSpec(qg.shape, const), pl.BlockSpec(kvg.shape, const),
                  pl.BlockSpec(w1.shape, const), pl.BlockSpec(w2.shape, const),
                  pl.BlockSpec(wkv.shape, const)],
        out_specs=(pl.BlockSpec((tm, head_cols), row),
                   pl.BlockSpec((tm, head_cols), row),
                   pl.BlockSpec((tm, head_cols), row),
                   pl.BlockSpec((tm, kv_lora), row),
                   pl.BlockSpec((tm, ROPE_DIM), row)),
        out_shape=(jax.ShapeDtypeStruct((T, head_cols), BF16),
                   jax.ShapeDtypeStruct((T, head_cols), BF16),
                   jax.ShapeDtypeStruct((T, head_cols), BF16),
                   jax.ShapeDtypeStruct((T, kv_lora), F32),
                   jax.ShapeDtypeStruct((T, ROPE_DIM), F32)),
        compiler_params=_cparams("parallel"),
        name="latent_stage",
    )(p, cos, sin, qg, kvg, w1, w2, wkv)


ATTN_TK = 512
ATTN_SM_ROWS = 128
ATTN_PV_ROWS = 256


def _chunk_mask(s, q0, k0):
    qpos = q0 + lax.broadcasted_iota(jnp.int32, s.shape, 0)
    kpos = k0 + lax.broadcasted_iota(jnp.int32, s.shape, 1)
    return jnp.where(kpos // CHUNK <= qpos // CHUNK, s, -jnp.inf)


def _attn_out(acc_sc):
    return acc_sc[:, :V_DIM] / acc_sc[:, V_DIM:]


def _attn_aligned_kernel(q_ref, k_ref, v_ref, o_ref, m_sc, acc_sc, s0, s1, p_sc, *, tq, tk, q_pos0):
    row0 = q_pos0 + pl.program_id(2) * tq
    n_pairs = row0 // (2 * tk)
    n_groups = tq // ATTN_PV_ROWS
    lower = tuple(range(n_groups // 2))
    upper = tuple(range(n_groups // 2, n_groups))

    m_sc[...] = jnp.full(m_sc.shape, -jnp.inf, F32)
    acc_sc[...] = jnp.zeros(acc_sc.shape, F32)

    def qk(start, s_ref, rows=slice(None)):
        s_ref[rows, :] = lax.dot_general(q_ref[rows, :], k_ref[pl.ds(start, tk), :], (((1,), (1,)), ((), ())),
                                         preferred_element_type=F32)

    def sm_pv(start, s_ref, groups, masked):
        for g in groups:
            alphas = []
            for r in range(ATTN_PV_ROWS // ATTN_SM_ROWS):
                r0 = g * ATTN_PV_ROWS + r * ATTN_SM_ROWS
                rs = slice(r0, r0 + ATTN_SM_ROWS)
                s = s_ref[rs, :]
                if masked:
                    s = _chunk_mask(s, r0 % tk, 0)
                m_prev = m_sc[rs, :]
                m_new = jnp.maximum(m_prev, jnp.max(s, axis=1, keepdims=True))
                p_sc[rs, :] = jnp.exp2(s - m_new[:, :1]).astype(BF16)
                alphas.append(jnp.exp2(m_prev - m_new))
                m_sc[rs, :] = m_new
            gs = slice(g * ATTN_PV_ROWS, (g + 1) * ATTN_PV_ROWS)
            alpha = jnp.concatenate(alphas, axis=0)
            acc_sc[gs, :] = (jnp.concatenate([alpha, alpha], axis=1) * acc_sc[gs, :]
                             + jnp.dot(p_sc[gs, :], v_ref[pl.ds(start, tk), :], preferred_element_type=F32))

    qk(0, s0)

    def pair(t):
        a = pl.multiple_of(2 * t * tk, 2 * tk)
        qk(a + tk, s1)
        sm_pv(a, s0, lower + upper, False)
        qk(a + 2 * tk, s0)
        sm_pv(a + tk, s1, lower + upper, False)

    def two_pairs(t, c):
        pair(2 * t)
        pair(2 * t + 1)
        return c

    lax.fori_loop(0, n_pairs // 2, two_pairs, 0)

    @pl.when(n_pairs % 2 == 1)
    def _():
        pair(n_pairs - 1)

    d0 = pl.multiple_of(2 * n_pairs * tk, 2 * tk)
    qk(d0 + tk, s1, slice(tq // 2, tq))
    sm_pv(d0, s0, lower, True)
    sm_pv(d0, s0, upper, False)
    sm_pv(d0 + tk, s1, upper, True)
    o_ref[...] = _attn_out(acc_sc)


def _attn_general_kernel(*refs, tq, tk, sk, q_pos0, heads, n_new):
    if n_new:
        q_ref, k_ref, v_ref, kn_ref, vn_ref, o_ref, m_sc, acc_sc = refs
    else:
        q_ref, k_ref, v_ref, o_ref, m_sc, acc_sc = refs
    row0 = q_pos0 + pl.program_id(2) * tq
    first_vis = (row0 // CHUNK + 1) * CHUNK
    last_vis = ((row0 + tq - 1) // CHUNK + 1) * CHUNK
    kv_hi = jnp.minimum(last_vis, sk)
    n_full = sk // tk
    tail = sk - n_full * tk
    n_unmasked = jnp.minimum(first_vis, sk) // tk
    n_masked_end = jnp.minimum((kv_hi + tk - 1) // tk, n_full)

    m_sc[...] = jnp.full(m_sc.shape, -jnp.inf, F32)
    acc_sc[...] = jnp.zeros(acc_sc.shape, F32)

    def step(kr, vr, start, size, k0, masked):
        for h in range(heads):
            cols = slice(h * HEAD_PAD, (h + 1) * HEAD_PAD)
            s = lax.dot_general(q_ref[:, cols], kr[pl.ds(start, size), cols], (((1,), (1,)), ((), ())),
                                preferred_element_type=F32)
            if masked:
                s = _chunk_mask(s, row0, k0)
            m_prev = m_sc[h]
            m_new = jnp.maximum(m_prev, jnp.max(s, axis=1, keepdims=True))
            p = jnp.exp2(s - m_new[:, :1]).astype(BF16)
            alpha = jnp.exp2(m_prev - m_new)
            acc_sc[h] = (jnp.concatenate([alpha, alpha], axis=1) * acc_sc[h]
                         + jnp.dot(p, vr[pl.ds(start, size), cols], preferred_element_type=F32))
            m_sc[h] = m_new

    def loop(lo, hi, masked):
        def body(j, c):
            start = pl.multiple_of(j * tk, tk)
            step(k_ref, v_ref, start, tk, start, masked)
            return c
        lax.fori_loop(lo, hi, body, 0)

    loop(0, n_unmasked, False)
    loop(n_unmasked, n_masked_end, True)
    if tail:
        @pl.when(kv_hi > n_full * tk)
        def _():
            step(k_ref, v_ref, n_full * tk, tail, n_full * tk, True)
    if n_new:
        step(kn_ref, vn_ref, 0, n_new, sk, True)
    for h in range(heads):
        o_ref[:, h * V_DIM:(h + 1) * V_DIM] = _attn_out(acc_sc.at[h])


ATTN_KV_BLOCK_BYTES = 4 * 1024 * 1024


def attention(q, kc, ve, batch, n_heads, q_pos0, past=None):
    sq = q.shape[0] // batch
    if past is None:
        sk = kc.shape[0] // batch
        tk = min(ATTN_TK, sk)
        aligned = sq % (2 * tk) == 0 and q_pos0 % (2 * tk) == 0 and sk == q_pos0 + sq
    else:
        sk = past[0].shape[0] // batch
        assert q_pos0 == sk and kc.shape[0] == q.shape[0]
        tk = min(ATTN_TK, sk)
        aligned = False
    tq = 2 * tk if aligned else _tile(sq, ATTN_TK)
    nq = sq // tq
    kv_map = lambda b, h, i: (b, h)
    if aligned:
        heads = 1
        body = functools.partial(_attn_aligned_kernel, tq=tq, tk=tk, q_pos0=q_pos0)
        scratch = [pltpu.VMEM((tq, LANES), F32), pltpu.VMEM((tq, 2 * V_DIM), F32),
                   pltpu.VMEM((tq, tk), F32), pltpu.VMEM((tq, tk), F32), pltpu.VMEM((tq, tk), BF16)]
        operands = (q, kc, ve)
    else:
        heads = max(d for d in range(1, n_heads + 1)
                    if n_heads % d == 0 and (d == 1 or d * sk * HEAD_PAD * 2 <= ATTN_KV_BLOCK_BYTES))
        scratch = [pltpu.VMEM((heads, tq, LANES), F32), pltpu.VMEM((heads, tq, 2 * V_DIM), F32)]
        n_new = sq if past is not None else 0
        body = functools.partial(_attn_general_kernel, tq=tq, tk=tk, sk=sk, q_pos0=q_pos0, heads=heads,
                                 n_new=n_new)
        operands = (q,) + (tuple(past) + (kc, ve) if past is not None else (kc, ve))
    in_specs = [pl.BlockSpec((tq, heads * HEAD_PAD), lambda b, h, i: (b * nq + i, h)),
                pl.BlockSpec((sk, heads * HEAD_PAD), kv_map), pl.BlockSpec((sk, heads * HEAD_PAD), kv_map)]
    if past is not None:
        in_specs += [pl.BlockSpec((sq, heads * HEAD_PAD), kv_map), pl.BlockSpec((sq, heads * HEAD_PAD), kv_map)]
    return pl.pallas_call(
        body,
        grid=(batch, n_heads // heads, nq),
        in_specs=in_specs,
        out_specs=pl.BlockSpec((tq, heads * V_DIM), lambda b, h, i: (b * nq + i, h)),
        out_shape=jax.ShapeDtypeStruct((batch * sq, n_heads * V_DIM), F32),
        scratch_shapes=scratch,
        compiler_params=_cparams("parallel", "parallel", "arbitrary"),
        name="attention",
    )(*operands)


def _lru_kernel(u_ref, gate_ref, ga_ref, gb_ref, attn_ref, h0_ref, conv0_ref, cw_ref, cb_ref, wa_ref, ba_ref,
                wx_ref, bx_ref, lam_ref, merged_ref, hlast_ref, convlast_ref, ubuf, hc, *, ts, conv_w):
    pad = SUBLANES

    @pl.when(pl.program_id(2) == 0)
    def _():
        ubuf[...] = conv0_ref[0]
        hc[...] = h0_ref[0]

    u = u_ref[...]
    prev = ubuf[...]
    head_row = lax.broadcasted_iota(jnp.int32, prev.shape, 0)
    uc = cb_ref[...]
    for j in range(conv_w):
        k = conv_w - 1 - j
        if k == 0:
            shifted = u
        else:
            rolled = pltpu.roll(u, k, 0)
            head = jnp.where(head_row < k, pltpu.roll(prev, k, 0), rolled[:pad, :])
            shifted = jnp.concatenate([head, rolled[pad:, :]], axis=0)
        uc = uc + shifted * cw_ref[j:j + 1, :]
    ubuf[...] = u[ts - pad:, :]
    convlast_ref[0] = u[ts - pad:, :]

    ucb = uc.astype(BF16)
    bc = uc.shape[1]
    r_parts, i_parts = [], []
    for n in range(bc // LRU_BW):
        blk = ucb[:, n * LRU_BW:(n + 1) * LRU_BW]
        r_parts.append(jnp.dot(blk, wa_ref[n], preferred_element_type=F32))
        i_parts.append(jnp.dot(blk, wx_ref[n], preferred_element_type=F32))
    r = jax.nn.sigmoid(jnp.concatenate(r_parts, axis=1) + ba_ref[...])
    ig = jax.nn.sigmoid(jnp.concatenate(i_parts, axis=1) + bx_ref[...])

    neg_lam = -lam_ref[...]
    softplus = jnp.maximum(neg_lam, 0.0) + jnp.log1p(jnp.exp(-jnp.abs(neg_lam)))
    log_a = (-LRU_C * r) * softplus
    a = jnp.exp(log_a)
    z = -jnp.tanh(log_a) * (a * a + 1.0)
    b = jnp.where(z > 0.0, z * lax.rsqrt(z), 0.0) * (ig * uc)

    grouped = (ts // SUBLANES, SUBLANES, a.shape[1])
    a = a.reshape(grouped)
    b = b.reshape(grouped)
    row_in_group = lax.broadcasted_iota(jnp.int32, grouped, 1)
    d = 1
    while d < SUBLANES:
        keep = row_in_group >= d
        a_prev = jnp.where(keep, pltpu.roll(a, d, 1), 1.0)
        b_prev = jnp.where(keep, pltpu.roll(b, d, 1), 0.0)
        b = a * b_prev + b
        a = a * a_prev
        d *= 2
    a = a.reshape(ts, grouped[2])
    b = b.reshape(ts, grouped[2])
    carry = hc[...]
    groups = []
    for g in range(ts // SUBLANES):
        rows = slice(g * SUBLANES, (g + 1) * SUBLANES)
        hg = b[rows, :] + a[rows, :] * carry
        carry = hg[SUBLANES - 1:SUBLANES, :]
        groups.append(hg)
    h = jnp.concatenate(groups, axis=0)
    hc[...] = carry
    hlast_ref[0] = carry
    merged = (jax.nn.sigmoid(ga_ref[...]) * (h * jax.nn.gelu(gate_ref[...]))
              + jax.nn.sigmoid(gb_ref[...]) * attn_ref[...])
    merged_ref[...] = merged.astype(BF16)


def lru_branch(pb, attn, h0, conv0p, cw, cb, wa, ba, wx, bx, lam, batch):
    T = pb.shape[0]
    W = pb.shape[1] // 4
    S = T // batch
    ts = _tile(S, 256)
    bc = min(W, 512)
    ns, nc = S // ts, W // bc
    nb = bc // LRU_BW
    conv_w = cw.shape[0]
    tile = lambda g: pl.BlockSpec((ts, bc), lambda b, c, s: (b * ns + s, g * nc + c))
    chan = lambda rows: pl.BlockSpec((rows, bc), lambda b, c, s: (0, c))
    state = lambda rows: pl.BlockSpec((1, rows, bc), lambda b, c, s: (b, 0, c))
    gatew = pl.BlockSpec((nb, LRU_BW, LRU_BW), lambda b, c, s: (c, 0, 0))
    return pl.pallas_call(
        functools.partial(_lru_kernel, ts=ts, conv_w=conv_w),
        grid=(batch, nc, ns),
        in_specs=[tile(0), tile(1), tile(2), tile(3), pl.BlockSpec((ts, bc), lambda b, c, s: (b * ns + s, c)),
                  state(1), state(SUBLANES), chan(conv_w), chan(1), gatew, chan(1), gatew, chan(1), chan(1)],
        out_specs=(pl.BlockSpec((ts, bc), lambda b, c, s: (b * ns + s, c)), state(1), state(SUBLANES)),
        out_shape=(jax.ShapeDtypeStruct((T, W), BF16),
                   jax.ShapeDtypeStruct((batch, 1, W), F32),
                   jax.ShapeDtypeStruct((batch, SUBLANES, W), F32)),
        scratch_shapes=[pltpu.VMEM((SUBLANES, bc), F32), pltpu.VMEM((1, bc), F32)],
        compiler_params=_cparams("parallel", "parallel", "arbitrary"),
        name="lru_branch",
    )(pb, pb, pb, pb, attn, h0, conv0p, cw, cb, wa, ba, wx, bx, lam)


def _pack_bf16_pairs(y):
    half = y.shape[1] // 2
    lo = pltpu.bitcast(y[:, :half].astype(BF16).astype(F32), jnp.uint32) >> 16
    hi = pltpu.bitcast(y[:, half:].astype(BF16).astype(F32), jnp.uint32) & jnp.uint32(0xFFFF0000)
    return lo | hi


def _unpack_bf16_pairs(w, xb_ref):
    half = w.shape[1]
    xb_ref[:, :half] = pltpu.bitcast(w << 16, F32).astype(BF16)
    xb_ref[:, half:] = pltpu.bitcast(w & jnp.uint32(0xFFFF0000), F32).astype(BF16)


def _merge_out_kernel(merged_ref, x_ref, wo_ref, g_ref, b_ref, y_ref, yb_ref, *, alpha, packed):
    out = jnp.dot(merged_ref[...], wo_ref[...], preferred_element_type=F32)
    y = _layer_norm_rows(alpha * x_ref[...] + out, g_ref[...], b_ref[...])
    y_ref[...] = y
    yb_ref[...] = _pack_bf16_pairs(y) if packed else y.astype(BF16)


def merge_out(merged, x, wo, g, b, alpha, packed):
    T, D = x.shape
    W = merged.shape[1]
    tm = _tile(T, 512)
    row = lambda i: (i, 0)
    second = ((T, D // 2), jnp.uint32) if packed else ((T, D), BF16)
    const = lambda i: (0, 0)
    return pl.pallas_call(
        functools.partial(_merge_out_kernel, alpha=alpha, packed=packed),
        grid=(T // tm,),
        in_specs=[pl.BlockSpec((tm, W), row), pl.BlockSpec((tm, D), row),
                  pl.BlockSpec(wo.shape, const), pl.BlockSpec((1, D), const), pl.BlockSpec((1, D), const)],
        out_specs=(pl.BlockSpec((tm, D), row), pl.BlockSpec((tm, second[0][1]), row)),
        out_shape=(jax.ShapeDtypeStruct((T, D), F32), jax.ShapeDtypeStruct(*second)),
        compiler_params=_cparams("parallel"),
        name="merge_out",
    )(merged, x, wo, g, b)


def _ffn_kernel(te_ref, nt_ref, x_ref, w1_ref, w3_ref, w2_ref, o_ref, *unpacked):
    del te_ref
    live = pl.program_id(0) < nt_ref[0]

    @pl.when(pl.program_id(1) == 0)
    def _():
        o_ref[...] = jnp.zeros(o_ref.shape, o_ref.dtype)

    if unpacked:
        @pl.when(live & (pl.program_id(1) == 0))
        def _():
            _unpack_bf16_pairs(x_ref[...], unpacked[0])

    @pl.when(live)
    def _():
        x = unpacked[0][...] if unpacked else x_ref[...]
        a = jnp.dot(x, w1_ref[0], preferred_element_type=F32)
        b = jnp.dot(x, w3_ref[0], preferred_element_type=F32)
        h = (a * jax.nn.sigmoid(a) * b).astype(BF16)
        o_ref[...] += jnp.dot(h, w2_ref[0], preferred_element_type=F32)


def grouped_ffn(xs, tile_expert, n_tiles, w1, w3, w2, tm):
    P = xs.shape[0]
    D = w1.shape[1]
    packed = xs.dtype == jnp.uint32
    F = w1.shape[2]
    tf = _tile(F, 512, LANES)
    nf = F // tf
    def live(i, nt):
        return jnp.minimum(i, nt[0] - 1)
    def ff(i, f, nt):
        return jnp.where(i < nt[0], f, nf - 1)
    grid_spec = pltpu.PrefetchScalarGridSpec(
        num_scalar_prefetch=2,
        grid=(P // tm, nf),
        in_specs=[pl.BlockSpec((tm, xs.shape[1]), lambda i, f, te, nt: (live(i, nt), 0)),
                  pl.BlockSpec((1, D, tf), lambda i, f, te, nt: (te[live(i, nt)], 0, ff(i, f, nt))),
                  pl.BlockSpec((1, D, tf), lambda i, f, te, nt: (te[live(i, nt)], 0, ff(i, f, nt))),
                  pl.BlockSpec((1, tf, D), lambda i, f, te, nt: (te[live(i, nt)], ff(i, f, nt), 0))],
        out_specs=pl.BlockSpec((tm, D), lambda i, f, te, nt: (i, 0)),
        scratch_shapes=[pltpu.VMEM((tm, D), BF16)] if packed else [],
    )
    return pl.pallas_call(
        _ffn_kernel,
        grid_spec=grid_spec,
        out_shape=jax.ShapeDtypeStruct((P, D), F32),
        compiler_params=_cparams("parallel", "arbitrary"),
        name="grouped_ffn",
    )(tile_expert, n_tiles, xs, w1, w3, w2)


def _resid_ln_kernel(x_ref, f_ref, g_ref, b_ref, y_ref, yb_ref, *, alpha):
    y = _layer_norm_rows(alpha * x_ref[...] + f_ref[...], g_ref[...], b_ref[...])
    y_ref[...] = y
    yb_ref[...] = y.astype(BF16)


def resid_ln(x, f, g, b, alpha):
    T, D = x.shape
    tm = _tile(T, 512)
    row = lambda i: (i, 0)
    const = lambda i: (0, 0)
    return pl.pallas_call(
        functools.partial(_resid_ln_kernel, alpha=alpha),
        grid=(T // tm,),
        in_specs=[pl.BlockSpec((tm, D), row), pl.BlockSpec((tm, D), row),
                  pl.BlockSpec((1, D), const), pl.BlockSpec((1, D), const)],
        out_specs=(pl.BlockSpec((tm, D), row), pl.BlockSpec((tm, D), row)),
        out_shape=(jax.ShapeDtypeStruct((T, D), F32), jax.ShapeDtypeStruct((T, D), BF16)),
        compiler_params=_cparams("parallel"),
        name="resid_ln",
    )(x, f, g, b)


def _router_kernel(x_ref, w_ref, idx_ref, gate_ref, cnt_ref, run_sc, *, n_experts):
    @pl.when(pl.program_id(0) == 0)
    def _():
        run_sc[...] = jnp.zeros(run_sc.shape, F32)

    logits = jnp.dot(x_ref[...], w_ref[...], preferred_element_type=F32, precision=lax.Precision.HIGHEST)
    tm = logits.shape[0]
    lane = lax.broadcasted_iota(jnp.int32, logits.shape, 1)
    logits = jnp.where(lane < n_experts, logits, -jnp.inf)
    m1 = jnp.max(logits, axis=1, keepdims=True)
    i1 = jnp.min(jnp.where(logits == m1, lane, LANES), axis=1, keepdims=True)
    rest = jnp.where(lane == i1, -jnp.inf, logits)
    m2 = jnp.max(rest, axis=1, keepdims=True)
    i2 = jnp.min(jnp.where(rest == m2, lane, LANES), axis=1, keepdims=True)
    e2 = jnp.exp(m2 - m1)
    g1 = 1.0 / (1.0 + e2)
    g2 = e2 / (1.0 + e2)
    gate_ref[...] = jnp.where(lane == 0, g1, jnp.where(lane == 1, g2, 0.0))

    hot1 = jnp.where(lane == i1, 1.0, 0.0)
    hot2 = jnp.where(lane == i2, 1.0, 0.0)
    before = (lax.broadcasted_iota(jnp.int32, (tm, tm), 1) < lax.broadcasted_iota(jnp.int32, (tm, tm), 0))
    before = jnp.where(before, 1.0, 0.0).astype(BF16)
    c1 = jnp.dot(before, hot1.astype(BF16), preferred_element_type=F32)
    c2 = jnp.dot(before, hot2.astype(BF16), preferred_element_type=F32)
    n1 = jnp.sum(hot1, axis=0, keepdims=True)
    n2 = jnp.sum(hot2, axis=0, keepdims=True)
    run = run_sc[...]
    rank1 = jnp.sum(hot1 * (c1 + run), axis=1, keepdims=True).astype(jnp.int32)
    rank2 = jnp.sum(hot2 * (c2 + run + n1), axis=1, keepdims=True).astype(jnp.int32)
    run_sc[...] = run + n1 + n2
    cnt_ref[...] = run_sc[...].astype(jnp.int32)
    idx_ref[...] = jnp.where(lane == 0, i1, jnp.where(lane == 1, i2,
                             jnp.where(lane == 2, rank1, jnp.where(lane == 3, rank2, 0))))


def router(x, w_router):
    T, D = x.shape
    E = w_router.shape[1]
    wp = jnp.pad(w_router, ((0, 0), (0, LANES - E)))
    tm = _tile(T, 512)
    row = lambda i: (i, 0)
    return pl.pallas_call(
        functools.partial(_router_kernel, n_experts=E),
        grid=(T // tm,),
        in_specs=[pl.BlockSpec((tm, D), row), pl.BlockSpec((D, LANES), lambda i: (0, 0))],
        out_specs=(pl.BlockSpec((tm, LANES), row), pl.BlockSpec((tm, LANES), row),
                   pl.BlockSpec((1, LANES), lambda i: (0, 0))),
        out_shape=(jax.ShapeDtypeStruct((T, LANES), jnp.int32), jax.ShapeDtypeStruct((T, LANES), F32),
                   jax.ShapeDtypeStruct((1, LANES), jnp.int32)),
        scratch_shapes=[pltpu.VMEM((1, LANES), F32)],
        compiler_params=_cparams("arbitrary"),
        name="router",
    )(x, wp)


DMA_ISSUE_UNROLL = 8


def _gather_kernel(tok_ref, x_hbm, o_ref, buf, sem, *, tg):
    i = pl.program_id(0)

    def start_tile(tile, slot):
        def issue(r2, c):
            for half in range(2):
                r = half * (tg // 2) + r2
                pltpu.make_async_copy(x_hbm.at[pl.ds(tok_ref[tile * tg + r], 1)], buf.at[slot, pl.ds(r, 1)],
                                      sem.at[slot, half]).start(priority=half)
            return c
        lax.fori_loop(0, tg // 2, issue, 0, unroll=DMA_ISSUE_UNROLL // 2)

    @pl.when(i == 0)
    def _():
        start_tile(0, 0)

    @pl.when(i + 1 < pl.num_programs(0))
    def _():
        start_tile(i + 1, (i + 1) % 2)

    slot = i % 2
    for half in range(2):
        rows = pl.ds(half * (tg // 2), tg // 2)
        pltpu.make_async_copy(x_hbm.at[pl.ds(0, tg // 2)], buf.at[slot, rows], sem.at[slot, half]).wait()
    o_ref[...] = buf[slot]


def gather_rows(x, row_token, tg):
    P = row_token.shape[0]
    D = x.shape[1]
    grid_spec = pltpu.PrefetchScalarGridSpec(
        num_scalar_prefetch=1,
        grid=(P // tg,),
        in_specs=[pl.BlockSpec(memory_space=pl.ANY)],
        out_specs=pl.BlockSpec((tg, D), lambda i, tok: (i, 0)),
        scratch_shapes=[pltpu.VMEM((2, tg, D), x.dtype), pltpu.SemaphoreType.DMA((2, 2))],
    )
    return pl.pallas_call(
        functools.partial(_gather_kernel, tg=tg),
        grid_spec=grid_spec,
        out_shape=jax.ShapeDtypeStruct((P, D), x.dtype),
        compiler_params=_cparams("arbitrary"),
        name="gather_rows",
    )(row_token, x)


def _combine_kernel(pos_ref, ys_hbm, gate_ref, x_ref, g_ref, b_ref, y_ref, yb_ref, buf, sem, *, tc, alpha):
    i = pl.program_id(0)

    def start_tile(tile, slot):
        def issue(r, c):
            p = (tile * tc + r) * TOP_K
            for k in range(TOP_K):
                pltpu.make_async_copy(ys_hbm.at[pl.ds(pos_ref[p + k], 1)], buf.at[slot, k, pl.ds(r, 1)],
                                      sem.at[slot, k]).start(priority=k % 2)
            return c
        lax.fori_loop(0, tc, issue, 0, unroll=DMA_ISSUE_UNROLL)

    @pl.when(i == 0)
    def _():
        start_tile(0, 0)

    @pl.when(i + 1 < pl.num_programs(0))
    def _():
        start_tile(i + 1, (i + 1) % 2)

    slot = i % 2
    for k in range(TOP_K):
        pltpu.make_async_copy(ys_hbm.at[pl.ds(0, tc)], buf.at[slot, k], sem.at[slot, k]).wait()
    gates = gate_ref[...]
    f = gates[:, 0:1] * buf[slot, 0] + gates[:, 1:2] * buf[slot, 1]
    y = _layer_norm_rows(alpha * x_ref[...] + f, g_ref[...], b_ref[...])
    y_ref[...] = y
    yb_ref[...] = y.astype(BF16)


def combine_ln(ys, pos, gates, x, g, b, alpha):
    T, D = x.shape
    tc = _tile(T, 256)
    row = lambda i, pos: (i, 0)
    const = lambda i, pos: (0, 0)
    grid_spec = pltpu.PrefetchScalarGridSpec(
        num_scalar_prefetch=1,
        grid=(T // tc,),
        in_specs=[pl.BlockSpec(memory_space=pl.ANY), pl.BlockSpec((tc, LANES), row),
                  pl.BlockSpec((tc, D), row), pl.BlockSpec((1, D), const), pl.BlockSpec((1, D), const)],
        out_specs=(pl.BlockSpec((tc, D), row), pl.BlockSpec((tc, D), row)),
        scratch_shapes=[pltpu.VMEM((2, TOP_K, tc, D), F32), pltpu.SemaphoreType.DMA((2, TOP_K))],
    )
    return pl.pallas_call(
        functools.partial(_combine_kernel, tc=tc, alpha=alpha),
        grid_spec=grid_spec,
        out_shape=(jax.ShapeDtypeStruct((T, D), F32), jax.ShapeDtypeStruct((T, D), BF16)),
        compiler_params=_cparams("arbitrary"),
        name="combine_ln",
    )(pos, ys, gates, x, g, b)


def moe_block(x, xp, w_router, w1, w3, w2, g, b, alpha):
    T, D = x.shape
    E = w1.shape[0]
    tm = 1024 if T * TOP_K >= 8 * 1024 else LANES
    idx, gates, counts = router(x, w_router)
    e_flat = idx[:, :TOP_K].reshape(-1)
    rank = idx[:, TOP_K:2 * TOP_K].reshape(-1)
    counts = counts[0, :E]
    padded = ((counts + tm - 1) // tm) * tm
    ends = jnp.cumsum(padded)
    pos = (ends - padded)[e_flat] + rank
    P = T * TOP_K + E * tm
    row_token = jnp.zeros((P,), jnp.int32).at[pos].set(jnp.arange(T * TOP_K, dtype=jnp.int32) // TOP_K)
    tile_start = jnp.arange(P // tm, dtype=jnp.int32) * tm
    tile_expert = jnp.minimum(jnp.sum(ends[None, :] <= tile_start[:, None], axis=1), E - 1).astype(jnp.int32)
    n_tiles = (ends[-1:] // tm).astype(jnp.int32)

    xs = gather_rows(xp, row_token, _tile(tm, 512))
    ys = grouped_ffn(xs, tile_expert, n_tiles, w1, w3, w2, tm)
    return combine_ln(ys, pos.astype(jnp.int32), gates, x, g, b, alpha)


def dense_block(x, xb, w1, w3, w2, g, b, alpha):
    T = x.shape[0]
    tm = _tile(T, 1024)
    f = grouped_ffn(xb, jnp.zeros((T // tm,), jnp.int32), jnp.full((1,), T // tm, jnp.int32), w1, w3, w2, tm)
    return resid_ln(x, f, g, b, alpha)


def _rot_cols(w):
    return jnp.concatenate([-w[..., ROPE_HALF:], w[..., :ROPE_HALF]], axis=-1)


def _prep_layer(l, w_in, q_norm_g, w_uq, kv_norm_g, w_uk, w_uv, conv_w, conv_b, w_gate_a, b_gate_a,
                w_gate_x, b_gate_x, lru_lambda, w_o):
    D = w_in.shape[1]
    q_lora, kv_lora = q_norm_g.shape[1], kv_norm_g.shape[1]
    n_heads = w_uk.shape[2] // NOPE_DIM
    W = conv_w.shape[2]
    wi = w_in[l]
    o = q_lora + kv_lora
    kr = wi[:, o:o + ROPE_DIM]
    z = jnp.zeros((D, LANES - ROPE_DIM), F32)
    w_lat = jnp.concatenate([wi[:, :o], kr, z, _rot_cols(kr), z], axis=1).astype(BF16)
    w_b = wi[:, o + ROPE_DIM:].astype(BF16)
    uq = w_uq[l].reshape(q_lora, n_heads, QK_DIM)
    nope, ropew = uq[..., :NOPE_DIM], uq[..., NOPE_DIM:]
    zq = jnp.zeros((q_lora, n_heads, LANES - ROPE_DIM), F32)
    w1 = jnp.concatenate([nope, ropew, zq], axis=-1).reshape(q_lora, n_heads * HEAD_PAD).astype(BF16)
    w2 = jnp.concatenate([_rot_cols(ropew), zq], axis=-1).reshape(q_lora, n_heads * LANES).astype(BF16)
    wkv = jnp.concatenate([w_uk[l], w_uv[l]], axis=1).astype(BF16)
    return dict(
        w_lat=w_lat, w_b=w_b, qg=q_norm_g[l][None, :], kvg=kv_norm_g[l][None, :], w1=w1, w2=w2, wkv=wkv,
        n_heads=n_heads, cw=conv_w[l], cb=conv_b[l][None, :],
        wa=w_gate_a[l].astype(BF16), ba=b_gate_a[l].reshape(1, W),
        wx=w_gate_x[l].astype(BF16), bx=b_gate_x[l].reshape(1, W),
        lam=lru_lambda[l][None, :], wo=w_o[l].astype(BF16))


def _pad_ff(w1, w3, w2):
    F = w1.shape[-1]
    Fp = -(-F // 512) * 512
    if Fp != F:
        w1 = jnp.pad(w1, ((0, 0), (0, 0), (0, Fp - F)))
        w3 = jnp.pad(w3, ((0, 0), (0, 0), (0, Fp - F)))
        w2 = jnp.pad(w2, ((0, 0), (0, Fp - F), (0, 0)))
    return w1.astype(BF16), w3.astype(BF16), w2.astype(BF16)


def kernel(x_prompt, x_sample, cache_kv_latent, cache_k_rope, state_lru, state_conv, w_in, q_norm_g, w_uq, kv_norm_g, w_uk, w_uv, conv_w, conv_b, w_gate_a, b_gate_a, w_gate_x, b_gate_x, lru_lambda, w_o, ln1_g, ln1_b, ln2_g, ln2_b, ffn_w1, ffn_w3, ffn_w2, router_w, moe_w1, moe_w3, moe_w2):
    depth = w_in.shape[0]
    alpha = (2 * depth) ** 0.25
    B, S, D = x_prompt.shape
    Bs, Ss, _ = x_sample.shape
    past = cache_kv_latent.shape[2]
    W = conv_w.shape[2]
    conv_pad = SUBLANES - (conv_w.shape[1] - 1)

    layers = [_prep_layer(l, w_in, q_norm_g, w_uq, kv_norm_g, w_uk, w_uv, conv_w, conv_b, w_gate_a, b_gate_a,
                          w_gate_x, b_gate_x, lru_lambda, w_o) for l in range(depth)]
    ffn = []
    for l in range(depth):
        if l % 2 == 0:
            ffn.append(_pad_ff(ffn_w1[l // 2][None], ffn_w3[l // 2][None], ffn_w2[l // 2][None]))
        else:
            ffn.append(_pad_ff(moe_w1[l // 2], moe_w3[l // 2], moe_w2[l // 2]))

    cos_p, sin_p = rope_tables(S, 0)
    cos_s, sin_s = rope_tables(Ss, past)
    cos_s, sin_s = jnp.tile(cos_s, (Bs, 1)), jnp.tile(sin_s, (Bs, 1))

    def trunk(x3, cos, sin, pos0, past_lat, past_kr, h0, conv0):
        Bg, Sg, _ = x3.shape
        x = x3.reshape(Bg * Sg, D)
        xb = x
        lats, krs, hs, bufs = [], [], [], []
        for l in range(depth):
            lw = layers[l]
            H = lw["n_heads"]
            p_lat = matmul(xb, lw["w_lat"], F32, tn_pref=lw["w_lat"].shape[1])
            pb = matmul(xb, lw["w_b"], F32)
            q, kc, ve, lat, kr = latent_stage(p_lat, cos, sin, lw["qg"], lw["kvg"], lw["w1"], lw["w2"],
                                              lw["wkv"], H)
            if past_lat is None:
                past_kv = None
                h_in = jnp.zeros((Bg, 1, W), F32)
                c_in = jnp.zeros((Bg, SUBLANES, W), F32)
            else:
                n_past = past_lat.shape[2]
                kr_past = jnp.pad(past_kr[l], ((0, 0), (0, 0), (0, LANES - ROPE_DIM)))
                past_kv = cache_kv(past_lat[l].reshape(Bg * n_past, -1), kr_past.reshape(Bg * n_past, LANES),
                                   lw["wkv"], H)
                h_in = h0[l][:, None, :]
                c_in = jnp.pad(conv0[l], ((0, 0), (conv_pad, 0), (0, 0)))
            attn = attention(q, kc, ve, Bg, H, pos0, past_kv)
            merged, h_last, conv_last = lru_branch(pb, attn, h_in, c_in, lw["cw"], lw["cb"], lw["wa"], lw["ba"],
                                                   lw["wx"], lw["bx"], lw["lam"], Bg)
            x, xb = merge_out(merged, x, lw["wo"], ln1_g[l][None, :], ln1_b[l][None, :], alpha,
                              packed=l % 2 == 1)
            w1, w3, w2 = ffn[l]
            if l % 2 == 0:
                x, xb = dense_block(x, xb, w1, w3, w2, ln2_g[l][None, :], ln2_b[l][None, :], alpha)
            else:
                x, xb = moe_block(x, xb, router_w[l // 2], w1, w3, w2, ln2_g[l][None, :], ln2_b[l][None, :],
                                  alpha)
            lats.append(lat.reshape(Bg, Sg, -1))
            krs.append(kr.reshape(Bg, Sg, ROPE_DIM))
            hs.append(h_last[:, 0, :])
            bufs.append(conv_last[:, conv_pad:, :])
        return x.reshape(Bg, Sg, D), jnp.stack(lats), jnp.stack(krs), jnp.stack(hs), jnp.stack(bufs)

    y_p, p_lat, p_kr, p_lru, p_conv = trunk(x_prompt, cos_p, sin_p, 0, None, None, None, None)
    y_s, s_lat, s_kr, s_lru, s_conv = trunk(x_sample, cos_s, sin_s, past, cache_kv_latent, cache_k_rope,
                                            state_lru, state_conv)
    return (y_p, y_s, p_lat, p_kr, p_lru, p_conv, s_lat, s_kr, s_lru, s_conv)
```
